```python
import jax
import jax.numpy as jnp
from jax import lax
import numpy as np

D_MODEL = 4096
BATCH = 2
SEQ = 8192
DEPTH = 2

CHUNK = 64
QBLOCK = 128
ROPE_THETA = 10000.0
NORM_EPS = 1e-6

A_HEADS = 16
A_HEAD_DIM = 128
A_Q_RANK = 1024
A_KV_RANK = 512
IDX_HEADS = 32
IDX_DIM = 128
IDX_TOPK_MAX = 256

B_HEADS = 32
B_HEAD_DIM = 64
B_WIDTH = B_HEADS * B_HEAD_DIM
B_DECAY_RANK = 128
B_ICLR_RANK = 128
B_GATE_RANK = 480
B_GN_EPS = 64e-5

A_IN = A_Q_RANK + A_KV_RANK + IDX_DIM + IDX_HEADS
B_IN = 3 * B_WIDTH + B_DECAY_RANK + B_ICLR_RANK + B_GATE_RANK
MIX_IN = A_IN + B_IN
MIX_WIDTH = A_HEADS * A_HEAD_DIM + B_WIDTH

C_INNER = 2 * D_MODEL
C_HEAD_DIM = 64
C_HEADS = C_INNER // C_HEAD_DIM
C_GROUPS = 8
C_HEADS_PER_GROUP = C_HEADS // C_GROUPS
C_STATE = 128
C_CONV = 4
C_CONV_DIM = C_INNER + 2 * C_GROUPS * C_STATE
C_IN = C_INNER + C_CONV_DIM + C_HEADS

FFN_DIM = 11008
N_EXPERTS = 8
TOP_K_EXPERTS = 2
EXPERT_DIM = 4096

kernel_name = 'hybrid_dsa_rwkv7_mamba2_moe_trunk'


def rmsnorm(x, g):
    xf = x.astype(jnp.float32)
    y = xf * lax.rsqrt(jnp.mean(xf * xf, axis=-1, keepdims=True) + NORM_EPS)
    return (y * g.astype(jnp.float32)).astype(x.dtype)


def rope_tables(positions, dim):
    inv_freq = ROPE_THETA ** (-jnp.arange(0, dim, 2, dtype=jnp.float32) / dim)
    ang = positions.astype(jnp.float32)[..., None] * inv_freq
    return jnp.cos(ang)[:, :, None, :], jnp.sin(ang)[:, :, None, :]


def apply_rope(x, cos, sin):
    xf = x.astype(jnp.float32)
    x1, x2 = jnp.split(xf, 2, axis=-1)
    return jnp.concatenate([x1 * cos - x2 * sin, x2 * cos + x1 * sin], axis=-1).astype(x.dtype)


def swiglu(h, w_gate, w_up, w_down):
    return (jax.nn.silu(h @ w_gate) * (h @ w_up)) @ w_down


def dsa_sparse_attention(q, q_idx, w_idx, k, v, k_idx):
    bsz, seq = q.shape[:2]
    n_blocks = seq // QBLOCK
    top_k = min(IDX_TOPK_MAX, seq // 4)
    scale = A_HEAD_DIM ** -0.5
    key_chunk = jnp.arange(seq) // CHUNK

    def to_blocks(t):
        return jnp.moveaxis(t.reshape(bsz, n_blocks, QBLOCK, *t.shape[2:]), 1, 0)

    def one_block(args):
        qb, qib, wb, start = args
        q_chunk = (start + jnp.arange(QBLOCK)) // CHUNK
        admissible = key_chunk[None, :] <= q_chunk[:, None]
        rel = jax.nn.relu(jnp.einsum('bqhd,bsd->bqhs', qib, k_idx).astype(jnp.float32))
        score = jnp.einsum('bqhs,bqh->bqs', rel, wb.astype(jnp.float32))
        score = jnp.where(admissible[None], score, -jnp.inf)
        _, sel = lax.top_k(score, top_k)
        valid = (sel // CHUNK) <= q_chunk[None, :, None]
        k_sel = jax.vmap(lambda kb, ib: kb[ib])(k, sel)
        v_sel = jax.vmap(lambda vb, ib: vb[ib])(v, sel)
        logits = jnp.einsum('bqhd,bqkd->bqhk', qb, k_sel).astype(jnp.float32) * scale
        logits = jnp.where(valid[:, :, None, :], logits, -jnp.inf)
        probs = jax.nn.softmax(logits, axis=-1).astype(v.dtype)
        return jnp.einsum('bqhk,bqkd->bqhd', probs, v_sel)

    starts = jnp.arange(n_blocks) * QBLOCK
    out = lax.map(one_block, (to_blocks(q), to_blocks(q_idx), to_blocks(w_idx), starts))
    return jnp.moveaxis(out, 0, 1).reshape(bsz, seq, A_HEADS * A_HEAD_DIM)


def wkv7_scan(r, w, k, v, a, b):
    bsz, seq, nh, hd = r.shape

    def step(state, inp):
        rt, wt, kt, vt, at, bt = inp
        sa = jnp.einsum('bhvk,bhk->bhv', state, at)
        state = state * wt[:, :, None, :] + sa[..., None] * bt[:, :, None, :] + vt[..., None] * kt[:, :, None, :]
        return state, jnp.einsum('bhvk,bhk->bhv', state, rt)

    xs = tuple(jnp.moveaxis(t, 1, 0) for t in (r, w, k, v, a, b))
    state0 = jnp.zeros((bsz, nh, hd, hd), jnp.float32)
    _, ys = lax.scan(step, state0, xs)
    return jnp.moveaxis(ys, 0, 1)


def rwkv7_time_mix(p, shift_mu, w_up, w0, a_up, a0, g_up, k_k, k_a, r_k, gn_w, gn_b):
    bsz, seq, _ = p.shape
    f32 = jnp.float32
    p_prev = jnp.pad(p[:, :-1], ((0, 0), (1, 0), (0, 0)))
    p = p + (p_prev - p) * shift_mu
    r, k, v, xw, xa, xg = jnp.split(
        p, [B_WIDTH, 2 * B_WIDTH, 3 * B_WIDTH, 3 * B_WIDTH + B_DECAY_RANK,
            3 * B_WIDTH + B_DECAY_RANK + B_ICLR_RANK], axis=-1)
    w_log = -jax.nn.softplus(-(w0 + jnp.tanh(xw) @ w_up).astype(f32)) - 0.5
    a = jax.nn.sigmoid((a0 + xa @ a_up).astype(f32))
    g = jax.nn.sigmoid(xg) @ g_up
    heads = lambda t: t.astype(f32).reshape(bsz, seq, B_HEADS, B_HEAD_DIM)
    r, k, v, a = heads(r), heads(k), heads(v), heads(a)
    decay = heads(jnp.exp(-jnp.exp(w_log)))
    kk = k * k_k.astype(f32).reshape(B_HEADS, B_HEAD_DIM)
    kk = kk * lax.rsqrt(jnp.maximum(jnp.sum(kk * kk, axis=-1, keepdims=True), 1e-24))
    k = k * (1.0 + (a - 1.0) * k_a.astype(f32).reshape(B_HEADS, B_HEAD_DIM))
    y = wkv7_scan(r, decay, k, v, -kk, kk * a)
    mean = jnp.mean(y, axis=-1, keepdims=True)
    var = jnp.mean(jnp.square(y - mean), axis=-1, keepdims=True)
    y = ((y - mean) * lax.rsqrt(var + B_GN_EPS)).reshape(bsz, seq, B_WIDTH) * gn_w.astype(f32) + gn_b.astype(f32)
    bonus = jnp.sum(r * k * r_k.astype(f32), axis=-1, keepdims=True) * v
    y = y + bonus.reshape(bsz, seq, B_WIDTH)
    return (y * g.astype(f32)).astype(p.dtype)


def ssd_chunked_scan(x, dt, a, b_in, c_out):
    bsz, seq, ng, hg, hp = x.shape
    ns = b_in.shape[-1]
    n_chunks = seq // CHUNK
    causal = jnp.tril(jnp.ones((CHUNK, CHUNK), dtype=bool))

    def to_chunks(t):
        return jnp.moveaxis(t.astype(jnp.float32).reshape(bsz, n_chunks, CHUNK, *t.shape[2:]), 1, 0)

    def step(state, inp):
        xc, dtc, bc, cc = inp
        cum = jnp.cumsum(dtc * a, axis=1)
        seg = cum[:, :, None] - cum[:, None, :]
        decay = jnp.exp(jnp.where(causal[None, :, :, None, None], seg, -jnp.inf))
        cb = jnp.einsum('btgn,bsgn->btsg', cc, bc)
        wts = cb[..., None] * decay * dtc[:, None]
        y = jnp.einsum('btsgh,bsghp->btghp', wts, xc)
        y = y + jnp.einsum('btgn,bghpn->btghp', cc, state) * jnp.exp(cum)[..., None]
        to_end = jnp.exp(cum[:, -1:] - cum) * dtc
        state = state * jnp.exp(cum[:, -1])[..., None, None] + jnp.einsum('bsgn,bsgh,bsghp->bghpn', bc, to_end, xc)
        return state, y

    state0 = jnp.zeros((bsz, ng, hg, hp, ns), jnp.float32)
    _, ys = lax.scan(step, state0, (to_chunks(x), to_chunks(dt), to_chunks(b_in), to_chunks(c_out)))
    return jnp.moveaxis(ys, 0, 1).reshape(bsz, seq, ng, hg, hp)


def mamba2_mixer(h, w_in, conv_w, conv_b, dt_bias, a_log, d_skip, gate_norm, w_out):
    bsz, seq, _ = h.shape
    f32 = jnp.float32
    proj = h @ w_in
    z, xbc, dt = jnp.split(proj, [C_INNER, C_INNER + C_CONV_DIM], axis=-1)
    xbc = lax.conv_general_dilated(xbc, conv_w[:, None, :], window_strides=(1,), padding=[(C_CONV - 1, 0)],
                                   dimension_numbers=('NWC', 'WIO', 'NWC'), feature_group_count=C_CONV_DIM)
    xbc = jax.nn.silu(xbc + conv_b)
    xs, b_in, c_out = jnp.split(xbc, [C_INNER, C_INNER + C_GROUPS * C_STATE], axis=-1)
    xs = xs.reshape(bsz, seq, C_GROUPS, C_HEADS_PER_GROUP, C_HEAD_DIM)
    b_in = b_in.reshape(bsz, seq, C_GROUPS, C_STATE)
    c_out = c_out.reshape(bsz, seq, C_GROUPS, C_STATE)
    dt = jax.nn.softplus(dt.astype(f32) + dt_bias.astype(f32)).reshape(bsz, seq, C_GROUPS, C_HEADS_PER_GROUP)
    a = -jnp.exp(a_log.astype(f32)).reshape(C_GROUPS, C_HEADS_PER_GROUP)
    y = ssd_chunked_scan(xs, dt, a, b_in, c_out)
    y = y + d_skip.astype(f32).reshape(C_GROUPS, C_HEADS_PER_GROUP)[..., None] * xs.astype(f32)
    y = y.reshape(bsz, seq, C_INNER) * jax.nn.silu(z.astype(f32))
    yg = y.reshape(bsz, seq, C_GROUPS, C_INNER // C_GROUPS)
    yg = yg * lax.rsqrt(jnp.mean(yg * yg, axis=-1, keepdims=True) + NORM_EPS)
    y = yg.reshape(bsz, seq, C_INNER) * gate_norm.astype(f32)
    return y.astype(h.dtype) @ w_out


def moe_swiglu(h, router, exp_gate, exp_up, exp_down):
    logits = jnp.einsum('btd,de->bte', h, router).astype(jnp.float32)
    top_vals, top_idx = lax.top_k(logits, TOP_K_EXPERTS)
    top_w = jax.nn.softmax(top_vals, axis=-1)
    gates = jnp.sum(jax.nn.one_hot(top_idx, N_EXPERTS, dtype=jnp.float32) * top_w[..., None], axis=-2)
    out = jnp.zeros_like(h)
    for e in range(N_EXPERTS):
        out = out + gates[..., e:e + 1].astype(h.dtype) * swiglu(h, exp_gate[e], exp_up[e], exp_down[e])
    return out


def dsa_rwkv_layer(x, positions, norm_mix, w_in, q_norm, w_uq, w_qi, kv_norm, w_uk, w_uv,
                   shift_mu, w_up, w0, a_up, a0, g_up, k_k, k_a, r_k, gn_w, gn_b, w_out,
                   norm_ffn, ffn_gate, ffn_up, ffn_down):
    bsz, seq, _ = x.shape
    cos_a, sin_a = rope_tables(positions, A_HEAD_DIM)
    cos_i, sin_i = rope_tables(positions, IDX_DIM)
    proj = rmsnorm(x, norm_mix) @ w_in
    c_q, c_kv, k_idx, w_idx, p_b = jnp.split(
        proj, [A_Q_RANK, A_Q_RANK + A_KV_RANK, A_Q_RANK + A_KV_RANK + IDX_DIM, A_IN], axis=-1)
    c_q = rmsnorm(c_q, q_norm)
    c_kv = rmsnorm(c_kv, kv_norm)
    q = apply_rope((c_q @ w_uq).reshape(bsz, seq, A_HEADS, A_HEAD_DIM), cos_a, sin_a)
    q_idx = apply_rope((c_q @ w_qi).reshape(bsz, seq, IDX_HEADS, IDX_DIM), cos_i, sin_i)
    k = apply_rope((c_kv @ w_uk)[:, :, None, :], cos_a, sin_a)[:, :, 0]
    v = c_kv @ w_uv
    k_idx = apply_rope(k_idx[:, :, None, :], cos_i, sin_i)[:, :, 0]
    w_idx = w_idx * (IDX_HEADS * IDX_DIM) ** -0.5
    y_a = dsa_sparse_attention(q, q_idx, w_idx, k, v, k_idx)
    y_b = rwkv7_time_mix(p_b, shift_mu, w_up, w0, a_up, a0, g_up, k_k, k_a, r_k, gn_w, gn_b)
    x = x + jnp.concatenate([y_a, y_b], axis=-1) @ w_out
    return x + swiglu(rmsnorm(x, norm_ffn), ffn_gate, ffn_up, ffn_down)


def ssd_moe_layer(x, norm_mix, w_in, conv_w, conv_b, dt_bias, a_log, d_skip, gate_norm, w_out,
                  norm_ffn, router, exp_gate, exp_up, exp_down):
    x = x + mamba2_mixer(rmsnorm(x, norm_mix), w_in, conv_w, conv_b, dt_bias, a_log, d_skip, gate_norm, w_out)
    return x + moe_swiglu(rmsnorm(x, norm_ffn), router, exp_gate, exp_up, exp_down)


def setup_inputs(seed: int = 0) -> dict:
    key = jax.random.key(seed)
    ks = iter(jax.random.split(key, 64))
    f32 = jnp.float32

    def normal(shape, scale):
        return jax.random.normal(next(ks), shape, f32) * scale

    def gain(n):
        return 1.0 + 0.02 * jax.random.normal(next(ks), (n,), f32)

    def uniform(shape, lo, hi):
        return jax.random.uniform(next(ks), shape, f32, lo, hi)

    x = normal((BATCH, SEQ, D_MODEL), 1.0)
    start = jax.random.randint(next(ks), (BATCH, 1), 0, 4096, dtype=jnp.int32)
    positions = start + jnp.arange(SEQ, dtype=jnp.int32)[None, :]
    dt0 = jnp.exp(uniform((C_HEADS,), float(np.log(1e-3)), float(np.log(1e-1))))
    dt_bias = dt0 + jnp.log(-jnp.expm1(-dt0))
    return {
        'x': x,
        'positions': positions,
        'l0_norm_mix': gain(D_MODEL),
        'l0_w_in': normal((D_MODEL, MIX_IN), D_MODEL ** -0.5),
        'l0_q_norm': gain(A_Q_RANK),
        'l0_w_uq': normal((A_Q_RANK, A_HEADS * A_HEAD_DIM), A_Q_RANK ** -0.5),
        'l0_w_qi': normal((A_Q_RANK, IDX_HEADS * IDX_DIM), A_Q_RANK ** -0.5),
        'l0_kv_norm': gain(A_KV_RANK),
        'l0_w_uk': normal((A_KV_RANK, A_HEAD_DIM), A_KV_RANK ** -0.5),
        'l0_w_uv': normal((A_KV_RANK, A_HEAD_DIM), A_KV_RANK ** -0.5),
        'l0_shift_mu': uniform((B_IN,), 0.0, 1.0),
        'l0_w_up': normal((B_DECAY_RANK, B_WIDTH), 0.5 * B_DECAY_RANK ** -0.5),
        'l0_w0': uniform((B_WIDTH,), -6.0, -1.0),
        'l0_a_up': normal((B_ICLR_RANK, B_WIDTH), B_ICLR_RANK ** -0.5),
        'l0_a0': normal((B_WIDTH,), 0.1),
        'l0_g_up': normal((B_GATE_RANK, B_WIDTH), B_GATE_RANK ** -0.5),
        'l0_k_k': 0.85 + normal((B_WIDTH,), 0.02),
        'l0_k_a': gain(B_WIDTH),
        'l0_r_k': normal((B_HEADS, B_HEAD_DIM), 0.1),
        'l0_gn_w': gain(B_WIDTH),
        'l0_gn_b': normal((B_WIDTH,), 0.02),
        'l0_w_out': normal((MIX_WIDTH, D_MODEL), MIX_WIDTH ** -0.5),
        'l0_norm_ffn': gain(D_MODEL),
        'l0_ffn_gate': normal((D_MODEL, FFN_DIM), D_MODEL ** -0.5),
        'l0_ffn_up': normal((D_MODEL, FFN_DIM), D_MODEL ** -0.5),
        'l0_ffn_down': normal((FFN_DIM, D_MODEL), FFN_DIM ** -0.5),
        'l1_norm_mix': gain(D_MODEL),
        'l1_w_in': normal((D_MODEL, C_IN), D_MODEL ** -0.5),
        'l1_conv_w': normal((C_CONV, C_CONV_DIM), C_CONV ** -0.5),
        'l1_conv_b': normal((C_CONV_DIM,), 0.02),
        'l1_dt_bias': dt_bias,
        'l1_a_log': jnp.log(uniform((C_HEADS,), 1.0, 16.0)),
        'l1_d_skip': gain(C_HEADS),
        'l1_gate_norm': gain(C_INNER),
        'l1_w_out': normal((C_INNER, D_MODEL), C_INNER ** -0.5),
        'l1_norm_ffn': gain(D_MODEL),
        'l1_router': normal((D_MODEL, N_EXPERTS), D_MODEL ** -0.5),
        'l1_exp_gate': normal((N_EXPERTS, D_MODEL, EXPERT_DIM), D_MODEL ** -0.5),
        'l1_exp_up': normal((N_EXPERTS, D_MODEL, EXPERT_DIM), D_MODEL ** -0.5),
        'l1_exp_down': normal((N_EXPERTS, EXPERT_DIM, D_MODEL), EXPERT_DIM ** -0.5),
        'final_norm': gain(D_MODEL),
    }


def reference(x, positions, l0_norm_mix, l0_w_in, l0_q_norm, l0_w_uq, l0_w_qi, l0_kv_norm, l0_w_uk, l0_w_uv,
              l0_shift_mu, l0_w_up, l0_w0, l0_a_up, l0_a0, l0_g_up, l0_k_k, l0_k_a, l0_r_k, l0_gn_w, l0_gn_b,
              l0_w_out, l0_norm_ffn, l0_ffn_gate, l0_ffn_up, l0_ffn_down,
              l1_norm_mix, l1_w_in, l1_conv_w, l1_conv_b, l1_dt_bias, l1_a_log, l1_d_skip, l1_gate_norm, l1_w_out,
              l1_norm_ffn, l1_router, l1_exp_gate, l1_exp_up, l1_exp_down, final_norm):
    layer_params = [
        (l0_norm_mix, l0_w_in, l0_q_norm, l0_w_uq, l0_w_qi, l0_kv_norm, l0_w_uk, l0_w_uv,
         l0_shift_mu, l0_w_up, l0_w0, l0_a_up, l0_a0, l0_g_up, l0_k_k, l0_k_a, l0_r_k, l0_gn_w, l0_gn_b,
         l0_w_out, l0_norm_ffn, l0_ffn_gate, l0_ffn_up, l0_ffn_down),
        (l1_norm_mix, l1_w_in, l1_conv_w, l1_conv_b, l1_dt_bias, l1_a_log, l1_d_skip, l1_gate_norm, l1_w_out,
         l1_norm_ffn, l1_router, l1_exp_gate, l1_exp_up, l1_exp_down),
    ]
    for i in range(DEPTH):
        if i % 2 == 0:
            x = dsa_rwkv_layer(x, positions, *layer_params[i])
        else:
            x = ssd_moe_layer(x, *layer_params[i])
    return rmsnorm(x, final_norm)
```

```python
import functools

import jax
import jax.numpy as jnp
from jax import lax
from jax.experimental import pallas as pl
from jax.experimental.pallas import tpu as pltpu

F32 = jnp.float32
BF16 = jnp.bfloat16

CHUNK = 64
QBLOCK = 128
ROPE_THETA = 10000.0
NORM_EPS = 1e-6

A_HEADS = 16
A_HEAD_DIM = 128
A_Q_RANK = 1024
A_KV_RANK = 512
IDX_HEADS = 32
IDX_DIM = 128
IDX_TOPK_MAX = 256

B_HEADS = 32
B_HEAD_DIM = 64
B_WIDTH = B_HEADS * B_HEAD_DIM
B_DECAY_RANK = 128
B_ICLR_RANK = 128
B_GATE_RANK = 480
B_GN_EPS = 64e-5
A_IN = A_Q_RANK + A_KV_RANK + IDX_DIM + IDX_HEADS

C_HEAD_DIM = 64
C_GROUPS = 8
C_STATE = 128
C_CONV = 4

N_EXPERTS = 8
TOP_K_EXPERTS = 2

VMEM_LIMIT_BYTES = 56 * 1024 * 1024


def _round_up(n, m):
    return (n + m - 1) // m * m


def _pad_axis(a, axis, size):
    if a.shape[axis] == size:
        return a
    pads = [(0, 0)] * a.ndim
    pads[axis] = (0, size - a.shape[axis])
    return jnp.pad(a, pads)


def _rmsnorm_kernel(x_ref, g_ref, o_ref):
    x = x_ref[...].astype(F32)
    ms = jnp.mean(x * x, axis=-1, keepdims=True)
    o_ref[...] = (x * lax.rsqrt(ms + NORM_EPS) * g_ref[...]).astype(o_ref.dtype)


def rmsnorm_pallas(x, g, out_dtype=BF16, tm=256):
    m, d = x.shape
    tm = min(tm, m)
    return pl.pallas_call(
        _rmsnorm_kernel,
        grid=(m // tm,),
        in_specs=[pl.BlockSpec((tm, d), lambda i: (i, 0)),
                  pl.BlockSpec((1, d), lambda i: (0, 0))],
        out_specs=pl.BlockSpec((tm, d), lambda i: (i, 0)),
        out_shape=jax.ShapeDtypeStruct((m, d), out_dtype),
        compiler_params=pltpu.CompilerParams(dimension_semantics=("parallel",)),
        name="rmsnorm",
    )(x, g.reshape(1, d).astype(F32))


def _mm_kernel(x_ref, w_ref, *rest, nk, has_res):
    if has_res:
        res_ref, o_ref, acc_ref = rest
    else:
        o_ref, acc_ref = rest
        res_ref = None
    p = jnp.dot(x_ref[...], w_ref[...], preferred_element_type=F32)

    def finish(acc):
        if has_res:
            acc = acc + res_ref[...].astype(F32)
        o_ref[...] = acc.astype(o_ref.dtype)

    if nk == 1:
        finish(p)
        return
    k = pl.program_id(2)

    @pl.when(k == 0)
    def _():
        acc_ref[...] = p

    @pl.when(jnp.logical_and(k > 0, k < nk - 1))
    def _():
        acc_ref[...] += p

    @pl.when(k == nk - 1)
    def _():
        finish(acc_ref[...] + p)


def matmul_pallas(x, w, res=None, out_dtype=F32, tm=1024, tn=512, tk=None, name="matmul"):
    m, kdim = x.shape
    k2, n = w.shape
    assert kdim == k2
    tm = min(tm, m)
    tn = min(tn, n)
    tk = kdim if tk is None else min(tk, kdim)
    assert m % tm == 0 and n % tn == 0 and kdim % tk == 0, (x.shape, w.shape, tm, tn, tk)
    nk = kdim // tk
    in_specs = [pl.BlockSpec((tm, tk), lambda i, j, k: (i, k)),
                pl.BlockSpec((tk, tn), lambda i, j, k: (k, j))]
    args = [x, w]
    if res is not None:
        in_specs.append(pl.BlockSpec((tm, tn), lambda i, j, k: (i, j)))
        args.append(res)
    acc_shape = (tm, tn) if nk > 1 else (8, 128)
    return pl.pallas_call(
        functools.partial(_mm_kernel, nk=nk, has_res=res is not None),
        grid=(m // tm, n // tn, nk),
        in_specs=in_specs,
        out_specs=pl.BlockSpec((tm, tn), lambda i, j, k: (i, j)),
        out_shape=jax.ShapeDtypeStruct((m, n), out_dtype),
        scratch_shapes=[pltpu.VMEM(acc_shape, F32)],
        compiler_params=pltpu.CompilerParams(
            dimension_semantics=("parallel", "parallel", "arbitrary"),
            vmem_limit_bytes=VMEM_LIMIT_BYTES),
        name=name,
    )(*args)


def _glu_kernel(x_ref, wg_ref, wu_ref, *rest, has_gates):
    if has_gates:
        gates_ref, o_ref = rest
    else:
        (o_ref,) = rest
    x = x_ref[...]
    g = jnp.dot(x, wg_ref[...], preferred_element_type=F32)
    u = jnp.dot(x, wu_ref[...], preferred_element_type=F32)
    h = g * jax.nn.sigmoid(g) * u
    if has_gates:
        e = pl.program_id(1)
        gates = gates_ref[...]
        lane = lax.broadcasted_iota(jnp.int32, gates.shape, 1)
        col = jnp.sum(jnp.where(lane == e, gates, 0.0), axis=1, keepdims=True)
        h = h * col
    o_ref[...] = h.astype(o_ref.dtype)


def glu_pallas(x, wg, wu, gates=None, tm=1024, tn=512, name="glu"):
    m, kdim = x.shape
    ne, k2, n = wg.shape
    assert kdim == k2 and n % tn == 0 and m % tm == 0
    nj = n // tn
    in_specs = [pl.BlockSpec((tm, kdim), lambda i, e, j: (i, 0)),
                pl.BlockSpec((None, kdim, tn), lambda i, e, j: (e, 0, j)),
                pl.BlockSpec((None, kdim, tn), lambda i, e, j: (e, 0, j))]
    args = [x, wg, wu]
    if gates is not None:
        in_specs.append(pl.BlockSpec((tm, gates.shape[1]), lambda i, e, j: (i, 0)))
        args.append(gates)
    return pl.pallas_call(
        functools.partial(_glu_kernel, has_gates=gates is not None),
        grid=(m // tm, ne, nj),
        in_specs=in_specs,
        out_specs=pl.BlockSpec((tm, tn), lambda i, e, j: (i, e * nj + j)),
        out_shape=jax.ShapeDtypeStruct((m, ne * n), BF16),
        compiler_params=pltpu.CompilerParams(
            dimension_semantics=("parallel", "arbitrary", "arbitrary"),
            vmem_limit_bytes=VMEM_LIMIT_BYTES),
        name=name,
    )(*args)


def _rmsnorm_jax(x, g):
    xf = x.astype(F32)
    return xf * lax.rsqrt(jnp.mean(xf * xf, axis=-1, keepdims=True) + NORM_EPS) * g.astype(F32)


def _rope_tables(positions, dim):
    inv_freq = ROPE_THETA ** (-jnp.arange(0, dim, 2, dtype=F32) / dim)
    ang = positions.astype(F32)[..., None] * inv_freq
    return jnp.cos(ang)[:, :, None, :], jnp.sin(ang)[:, :, None, :]


def _apply_rope(x, cos, sin):
    x1, x2 = jnp.split(x.astype(F32), 2, axis=-1)
    return jnp.concatenate([x1 * cos - x2 * sin, x2 * cos + x1 * sin], axis=-1)


def _dsa_sparse_attention(q, q_idx, w_idx, k, v, k_idx):
    bsz, seq = q.shape[:2]
    n_blocks = seq // QBLOCK
    top_k = min(IDX_TOPK_MAX, seq // 4)
    scale = A_HEAD_DIM ** -0.5
    key_chunk = jnp.arange(seq) // CHUNK

    def to_blocks(t):
        return jnp.moveaxis(t.reshape(bsz, n_blocks, QBLOCK, *t.shape[2:]), 1, 0)

    def one_block(args):
        qb, qib, wb, start = args
        q_chunk = (start + jnp.arange(QBLOCK)) // CHUNK
        admissible = key_chunk[None, :] <= q_chunk[:, None]
        rel = jax.nn.relu(jnp.einsum('bqhd,bsd->bqhs', qib, k_idx).astype(F32))
        score = jnp.einsum('bqhs,bqh->bqs', rel, wb.astype(F32))
        score = jnp.where(admissible[None], score, -jnp.inf)
        _, sel = lax.top_k(score, top_k)
        valid = (sel // CHUNK) <= q_chunk[None, :, None]
        k_sel = jax.vmap(lambda kb, ib: kb[ib])(k, sel)
        v_sel = jax.vmap(lambda vb, ib: vb[ib])(v, sel)
        logits = jnp.einsum('bqhd,bqkd->bqhk', qb, k_sel).astype(F32) * scale
        logits = jnp.where(valid[:, :, None, :], logits, -jnp.inf)
        probs = jax.nn.softmax(logits, axis=-1).astype(v.dtype)
        return jnp.einsum('bqhk,bqkd->bqhd', probs, v_sel)

    starts = jnp.arange(n_blocks) * QBLOCK
    out = lax.map(one_block, (to_blocks(q), to_blocks(q_idx), to_blocks(w_idx), starts))
    return jnp.moveaxis(out, 0, 1).reshape(bsz, seq, A_HEADS * A_HEAD_DIM)


def _wkv7_scan(r, w, k, v, a, b):
    bsz, seq, nh, hd = r.shape

    def step(state, inp):
        rt, wt, kt, vt, at, bt = inp
        sa = jnp.einsum('bhvk,bhk->bhv', state, at)
        state = state * wt[:, :, None, :] + sa[..., None] * bt[:, :, None, :] + vt[..., None] * kt[:, :, None, :]
        return state, jnp.einsum('bhvk,bhk->bhv', state, rt)

    xs = tuple(jnp.moveaxis(t, 1, 0) for t in (r, w, k, v, a, b))
    state0 = jnp.zeros((bsz, nh, hd, hd), F32)
    _, ys = lax.scan(step, state0, xs)
    return jnp.moveaxis(ys, 0, 1)


def _mm3(x, w, **kw):
    bsz, seq, kdim = x.shape
    kp = _round_up(kdim, 128)
    n = w.shape[1]
    np_ = _round_up(n, 128)
    xb = _pad_axis(x.reshape(bsz * seq, kdim).astype(BF16), 1, kp)
    wb = _pad_axis(_pad_axis(w.astype(BF16), 0, kp), 1, np_)
    tn = kw.pop("tn", 512)
    while np_ % tn:
        tn //= 2
    out = matmul_pallas(xb, wb, tn=tn, **kw)
    return out[:, :n].reshape(bsz, seq, n)


def _rwkv7_time_mix(p, shift_mu, w_up, w0, a_up, a0, g_up, k_k, k_a, r_k, gn_w, gn_b):
    bsz, seq, _ = p.shape
    p_prev = jnp.pad(p[:, :-1], ((0, 0), (1, 0), (0, 0)))
    p = p + (p_prev - p) * shift_mu
    r, k, v, xw, xa, xg = jnp.split(
        p, [B_WIDTH, 2 * B_WIDTH, 3 * B_WIDTH, 3 * B_WIDTH + B_DECAY_RANK,
            3 * B_WIDTH + B_DECAY_RANK + B_ICLR_RANK], axis=-1)
    w_log = -jax.nn.softplus(-(w0 + _mm3(jnp.tanh(xw), w_up))) - 0.5
    a = jax.nn.sigmoid(a0 + _mm3(xa, a_up))
    g = _mm3(jax.nn.sigmoid(xg), g_up)
    heads = lambda t: t.astype(F32).reshape(bsz, seq, B_HEADS, B_HEAD_DIM)
    r, k, v, a = heads(r), heads(k), heads(v), heads(a)
    decay = heads(jnp.exp(-jnp.exp(w_log)))
    kk = k * k_k.astype(F32).reshape(B_HEADS, B_HEAD_DIM)
    kk = kk * lax.rsqrt(jnp.maximum(jnp.sum(kk * kk, axis=-1, keepdims=True), 1e-24))
    k = k * (1.0 + (a - 1.0) * k_a.astype(F32).reshape(B_HEADS, B_HEAD_DIM))
    y = _wkv7_scan(r, decay, k, v, -kk, kk * a)
    mean = jnp.mean(y, axis=-1, keepdims=True)
    var = jnp.mean(jnp.square(y - mean), axis=-1, keepdims=True)
    y = ((y - mean) * lax.rsqrt(var + B_GN_EPS)).reshape(bsz, seq, B_WIDTH) * gn_w.astype(F32) + gn_b.astype(F32)
    bonus = jnp.sum(r * k * r_k.astype(F32), axis=-1, keepdims=True) * v
    y = y + bonus.reshape(bsz, seq, B_WIDTH)
    return y * g


def _ssd_chunked_scan(x, dt, a, b_in, c_out):
    bsz, seq, ng, hg, hp = x.shape
    ns = b_in.shape[-1]
    n_chunks = seq // CHUNK
    causal = jnp.tril(jnp.ones((CHUNK, CHUNK), dtype=bool))

    def to_chunks(t):
        return jnp.moveaxis(t.astype(F32).reshape(bsz, n_chunks, CHUNK, *t.shape[2:]), 1, 0)

    def step(state, inp):
        xc, dtc, bc, cc = inp
        cum = jnp.cumsum(dtc * a, axis=1)
        seg = cum[:, :, None] - cum[:, None, :]
        decay = jnp.exp(jnp.where(causal[None, :, :, None, None], seg, -jnp.inf))
        cb = jnp.einsum('btgn,bsgn->btsg', cc, bc)
        wts = cb[..., None] * decay * dtc[:, None]
        y = jnp.einsum('btsgh,bsghp->btghp', wts, xc)
        y = y + jnp.einsum('btgn,bghpn->btghp', cc, state) * jnp.exp(cum)[..., None]
        to_end = jnp.exp(cum[:, -1:] - cum) * dtc
        state = state * jnp.exp(cum[:, -1])[..., None, None] + jnp.einsum('bsgn,bsgh,bsghp->bghpn', bc, to_end, xc)
        return state, y

    state0 = jnp.zeros((bsz, ng, hg, hp, ns), F32)
    _, ys = lax.scan(step, state0, (to_chunks(x), to_chunks(dt), to_chunks(b_in), to_chunks(c_out)))
    return jnp.moveaxis(ys, 0, 1).reshape(bsz, seq, ng, hg, hp)


def _dsa_rwkv_layer(x, positions, norm_mix, w_in, q_norm, w_uq, w_qi, kv_norm, w_uk, w_uv,
                    shift_mu, w_up, w0, a_up, a0, g_up, k_k, k_a, r_k, gn_w, gn_b, w_out,
                    norm_ffn, ffn_gate, ffn_up, ffn_down):
    bsz, seq, d = x.shape
    m = bsz * seq
    x2 = x.reshape(m, d)
    cos_a, sin_a = _rope_tables(positions, A_HEAD_DIM)
    cos_i, sin_i = _rope_tables(positions, IDX_DIM)

    mix_in = w_in.shape[1]
    w_in_b = _pad_axis(w_in.astype(BF16), 1, _round_up(mix_in, 512))
    proj = matmul_pallas(rmsnorm_pallas(x2, norm_mix), w_in_b, name="l0_in")[:, :mix_in]
    proj = proj.reshape(bsz, seq, mix_in)
    c_q, c_kv, k_idx, w_idx, p_b = jnp.split(
        proj, [A_Q_RANK, A_Q_RANK + A_KV_RANK, A_Q_RANK + A_KV_RANK + IDX_DIM, A_IN], axis=-1)
    c_q = _rmsnorm_jax(c_q, q_norm)
    c_kv = _rmsnorm_jax(c_kv, kv_norm)
    q = _apply_rope(_mm3(c_q, w_uq).reshape(bsz, seq, A_HEADS, A_HEAD_DIM), cos_a, sin_a)
    q_idx = _apply_rope(_mm3(c_q, w_qi).reshape(bsz, seq, IDX_HEADS, IDX_DIM), cos_i, sin_i)
    kv = _mm3(c_kv, jnp.concatenate([w_uk, w_uv], axis=1))
    k = _apply_rope(kv[:, :, None, :A_HEAD_DIM], cos_a, sin_a)[:, :, 0]
    v = kv[:, :, A_HEAD_DIM:]
    k_idx = _apply_rope(k_idx[:, :, None, :], cos_i, sin_i)[:, :, 0]
    w_idx = w_idx * (IDX_HEADS * IDX_DIM) ** -0.5
    y_a = _dsa_sparse_attention(q, q_idx, w_idx, k, v, k_idx)
    y_b = _rwkv7_time_mix(p_b, shift_mu, w_up, w0, a_up, a0, g_up, k_k, k_a, r_k, gn_w, gn_b)
    y = jnp.concatenate([y_a, y_b], axis=-1).reshape(m, -1).astype(BF16)
    x2 = matmul_pallas(y, w_out.astype(BF16), res=x2, name="l0_out")

    ffn = ffn_gate.shape[1]
    ffn_p = _round_up(ffn, 512)
    h = rmsnorm_pallas(x2, norm_ffn)
    wg = _pad_axis(ffn_gate.astype(BF16), 1, ffn_p)[None]
    wu = _pad_axis(ffn_up.astype(BF16), 1, ffn_p)[None]
    wd = _pad_axis(ffn_down.astype(BF16), 0, ffn_p)
    hh = glu_pallas(h, wg, wu, name="l0_glu")
    x2 = matmul_pallas(hh, wd, res=x2, tm=1024, tn=1024, tk=512, name="l0_down")
    return x2.reshape(bsz, seq, d)


def _mamba2_mixer(h2, bsz, seq, w_in, conv_w, conv_b, dt_bias, a_log, d_skip, gate_norm, w_out):
    c_inner = w_out.shape[0]
    c_heads = c_inner // C_HEAD_DIM
    hpg = c_heads // C_GROUPS
    c_conv_dim = c_inner + 2 * C_GROUPS * C_STATE
    c_in = w_in.shape[1]
    w_in_b = _pad_axis(w_in.astype(BF16), 1, _round_up(c_in, 512))
    proj = matmul_pallas(h2, w_in_b, name="l1_in")[:, :c_in].reshape(bsz, seq, c_in)
    z, xbc, dt = jnp.split(proj, [c_inner, c_inner + c_conv_dim], axis=-1)
    xbc = lax.conv_general_dilated(xbc, conv_w[:, None, :], window_strides=(1,), padding=[(C_CONV - 1, 0)],
                                   dimension_numbers=('NWC', 'WIO', 'NWC'), feature_group_count=c_conv_dim,
                                   precision=lax.Precision.HIGHEST)
    xbc = jax.nn.silu(xbc + conv_b)
    xs, b_in, c_out = jnp.split(xbc, [c_inner, c_inner + C_GROUPS * C_STATE], axis=-1)
    xs = xs.reshape(bsz, seq, C_GROUPS, hpg, C_HEAD_DIM)
    b_in = b_in.reshape(bsz, seq, C_GROUPS, C_STATE)
    c_out = c_out.reshape(bsz, seq, C_GROUPS, C_STATE)
    dt = jax.nn.softplus(dt.astype(F32) + dt_bias.astype(F32)).reshape(bsz, seq, C_GROUPS, hpg)
    a = -jnp.exp(a_log.astype(F32)).reshape(C_GROUPS, hpg)
    y = _ssd_chunked_scan(xs, dt, a, b_in, c_out)
    y = y + d_skip.astype(F32).reshape(C_GROUPS, hpg)[..., None] * xs.astype(F32)
    y = y.reshape(bsz, seq, c_inner) * jax.nn.silu(z.astype(F32))
    yg = y.reshape(bsz, seq, C_GROUPS, c_inner // C_GROUPS)
    yg = yg * lax.rsqrt(jnp.mean(yg * yg, axis=-1, keepdims=True) + NORM_EPS)
    y = yg.reshape(bsz, seq, c_inner) * gate_norm.astype(F32)
    return y.reshape(bsz * seq, c_inner).astype(BF16)


def _ssd_moe_layer(x, norm_mix, w_in, conv_w, conv_b, dt_bias, a_log, d_skip, gate_norm, w_out,
                   norm_ffn, router, exp_gate, exp_up, exp_down):
    bsz, seq, d = x.shape
    m = bsz * seq
    x2 = x.reshape(m, d)
    y = _mamba2_mixer(rmsnorm_pallas(x2, norm_mix), bsz, seq, w_in, conv_w, conv_b, dt_bias, a_log, d_skip,
                      gate_norm, w_out)
    x2 = matmul_pallas(y, w_out.astype(BF16), res=x2, tm=1024, tn=1024, tk=512, name="l1_out")

    h32 = rmsnorm_pallas(x2, norm_ffn, out_dtype=F32)
    logits = jnp.dot(h32, router, precision=lax.Precision.HIGHEST)
    top_vals, top_idx = lax.top_k(logits, TOP_K_EXPERTS)
    top_w = jax.nn.softmax(top_vals, axis=-1)
    gates = jnp.sum(jax.nn.one_hot(top_idx, N_EXPERTS, dtype=F32) * top_w[..., None], axis=-2)
    h = h32.astype(BF16)
    hh = glu_pallas(h, exp_gate.astype(BF16), exp_up.astype(BF16), gates=gates, name="l1_moe_glu")
    wd = exp_down.astype(BF16).reshape(-1, d)
    x2 = matmul_pallas(hh, wd, res=x2, tm=1024, tn=1024, tk=512, name="l1_moe_down")
    return x2.reshape(bsz, seq, d)


def kernel(x, positions, l0_norm_mix, l0_w_in, l0_q_norm, l0_w_uq, l0_w_qi, l0_kv_norm, l0_w_uk, l0_w_uv,
           l0_shift_mu, l0_w_up, l0_w0, l0_a_up, l0_a0, l0_g_up, l0_k_k, l0_k_a, l0_r_k, l0_gn_w, l0_gn_b,
           l0_w_out, l0_norm_ffn, l0_ffn_gate, l0_ffn_up, l0_ffn_down,
           l1_norm_mix, l1_w_in, l1_conv_w, l1_conv_b, l1_dt_bias, l1_a_log, l1_d_skip, l1_gate_norm, l1_w_out,
           l1_norm_ffn, l1_router, l1_exp_gate, l1_exp_up, l1_exp_down, final_norm):
    x = _dsa_rwkv_layer(x, positions, l0_norm_mix, l0_w_in, l0_q_norm, l0_w_uq, l0_w_qi, l0_kv_norm, l0_w_uk,
                        l0_w_uv, l0_shift_mu, l0_w_up, l0_w0, l0_a_up, l0_a0, l0_g_up, l0_k_k, l0_k_a, l0_r_k,
                        l0_gn_w, l0_gn_b, l0_w_out, l0_norm_ffn, l0_ffn_gate, l0_ffn_up, l0_ffn_down)
    x = _ssd_moe_layer(x, l1_norm_mix, l1_w_in, l1_conv_w, l1_conv_b, l1_dt_bias, l1_a_log, l1_d_skip,
                       l1_gate_norm, l1_w_out, l1_norm_ffn, l1_router, l1_exp_gate, l1_exp_up, l1_exp_down)
    bsz, seq, d = x.shape
    return rmsnorm_pallas(x.reshape(bsz * seq, d), final_norm, out_dtype=x.dtype).reshape(bsz, seq, d)
```

```python
import functools

import numpy as np
import jax
import jax.numpy as jnp
from jax import lax
from jax.experimental import pallas as pl
from jax.experimental.pallas import tpu as pltpu

F32 = jnp.float32
BF16 = jnp.bfloat16
I32 = jnp.int32
HI = lax.Precision.HIGHEST

CHUNK = 64
QBLOCK = 128
ROPE_THETA = 10000.0
NORM_EPS = 1e-6

A_HEADS = 16
A_HEAD_DIM = 128
A_Q_RANK = 1024
A_KV_RANK = 512
IDX_HEADS = 32
IDX_DIM = 128
IDX_TOPK_MAX = 256

B_HEADS = 32
B_HEAD_DIM = 64
B_WIDTH = B_HEADS * B_HEAD_DIM
B_DECAY_RANK = 128
B_ICLR_RANK = 128
B_GATE_RANK = 480
B_GN_EPS = 64e-5
A_IN = A_Q_RANK + A_KV_RANK + IDX_DIM + IDX_HEADS

C_HEAD_DIM = 64
C_GROUPS = 8
C_STATE = 128
C_CONV = 4

N_EXPERTS = 8
TOP_K_EXPERTS = 2

VMEM_LIMIT_BYTES = 56 * 1024 * 1024


def _round_up(n, m):
    return (n + m - 1) // m * m


def _pad_axis(a, axis, size):
    if a.shape[axis] == size:
        return a
    pads = [(0, 0)] * a.ndim
    pads[axis] = (0, size - a.shape[axis])
    return jnp.pad(a, pads)


def _rmsnorm_kernel(x_ref, g_ref, o_ref):
    x = x_ref[...].astype(F32)
    ms = jnp.mean(x * x, axis=-1, keepdims=True)
    o_ref[...] = (x * lax.rsqrt(ms + NORM_EPS) * g_ref[...]).astype(o_ref.dtype)


def rmsnorm_pallas(x, g, out_dtype=BF16, tm=256):
    m, d = x.shape
    tm = min(tm, m)
    return pl.pallas_call(
        _rmsnorm_kernel,
        grid=(m // tm,),
        in_specs=[pl.BlockSpec((tm, d), lambda i: (i, 0)),
                  pl.BlockSpec((1, d), lambda i: (0, 0))],
        out_specs=pl.BlockSpec((tm, d), lambda i: (i, 0)),
        out_shape=jax.ShapeDtypeStruct((m, d), out_dtype),
        compiler_params=pltpu.CompilerParams(dimension_semantics=("parallel",)),
        name="rmsnorm",
    )(x, g.reshape(1, d).astype(F32))


def _mm_kernel(x_ref, w_ref, *rest, nk, has_res):
    if has_res:
        res_ref, o_ref, acc_ref = rest
    else:
        o_ref, acc_ref = rest
        res_ref = None
    p = jnp.dot(x_ref[...], w_ref[...], preferred_element_type=F32)

    def finish(acc):
        if has_res:
            acc = acc + res_ref[...].astype(F32)
        o_ref[...] = acc.astype(o_ref.dtype)

    if nk == 1:
        finish(p)
        return
    k = pl.program_id(2)

    @pl.when(k == 0)
    def _():
        acc_ref[...] = p

    @pl.when(jnp.logical_and(k > 0, k < nk - 1))
    def _():
        acc_ref[...] += p

    @pl.when(k == nk - 1)
    def _():
        finish(acc_ref[...] + p)


def matmul_pallas(x, w, res=None, out_dtype=F32, tm=1024, tn=512, tk=None, name="matmul"):
    m, kdim = x.shape
    k2, n = w.shape
    assert kdim == k2
    tm = min(tm, m)
    tn = min(tn, n)
    tk = kdim if tk is None else min(tk, kdim)
    assert m % tm == 0 and n % tn == 0 and kdim % tk == 0, (x.shape, w.shape, tm, tn, tk)
    nk = kdim // tk
    in_specs = [pl.BlockSpec((tm, tk), lambda i, j, k: (i, k)),
                pl.BlockSpec((tk, tn), lambda i, j, k: (k, j))]
    args = [x, w]
    if res is not None:
        in_specs.append(pl.BlockSpec((tm, tn), lambda i, j, k: (i, j)))
        args.append(res)
    acc_shape = (tm, tn) if nk > 1 else (8, 128)
    return pl.pallas_call(
        functools.partial(_mm_kernel, nk=nk, has_res=res is not None),
        grid=(m // tm, n // tn, nk),
        in_specs=in_specs,
        out_specs=pl.BlockSpec((tm, tn), lambda i, j, k: (i, j)),
        out_shape=jax.ShapeDtypeStruct((m, n), out_dtype),
        scratch_shapes=[pltpu.VMEM(acc_shape, F32)],
        compiler_params=pltpu.CompilerParams(
            dimension_semantics=("parallel", "parallel", "arbitrary"),
            vmem_limit_bytes=VMEM_LIMIT_BYTES),
        name=name,
    )(*args)


def _glu_kernel(x_ref, wg_ref, wu_ref, *rest, has_gates):
    if has_gates:
        gates_ref, o_ref = rest
    else:
        (o_ref,) = rest
    x = x_ref[...]
    g = jnp.dot(x, wg_ref[...], preferred_element_type=F32)
    u = jnp.dot(x, wu_ref[...], preferred_element_type=F32)
    h = g * jax.nn.sigmoid(g) * u
    if has_gates:
        e = pl.program_id(1)
        gates = gates_ref[...]
        lane = lax.broadcasted_iota(jnp.int32, gates.shape, 1)
        col = jnp.sum(jnp.where(lane == e, gates, 0.0), axis=1, keepdims=True)
        h = h * col
    o_ref[...] = h.astype(o_ref.dtype)


def glu_pallas(x, wg, wu, gates=None, tm=1024, tn=512, name="glu"):
    m, kdim = x.shape
    ne, k2, n = wg.shape
    assert kdim == k2 and n % tn == 0 and m % tm == 0
    nj = n // tn
    in_specs = [pl.BlockSpec((tm, kdim), lambda i, e, j: (i, 0)),
                pl.BlockSpec((None, kdim, tn), lambda i, e, j: (e, 0, j)),
                pl.BlockSpec((None, kdim, tn), lambda i, e, j: (e, 0, j))]
    args = [x, wg, wu]
    if gates is not None:
        in_specs.append(pl.BlockSpec((tm, gates.shape[1]), lambda i, e, j: (i, 0)))
        args.append(gates)
    return pl.pallas_call(
        functools.partial(_glu_kernel, has_gates=gates is not None),
        grid=(m // tm, ne, nj),
        in_specs=in_specs,
        out_specs=pl.BlockSpec((tm, tn), lambda i, e, j: (i, e * nj + j)),
        out_shape=jax.ShapeDtypeStruct((m, ne * n), BF16),
        compiler_params=pltpu.CompilerParams(
            dimension_semantics=("parallel", "arbitrary", "arbitrary"),
            vmem_limit_bytes=VMEM_LIMIT_BYTES),
        name=name,
    )(*args)


def _rmsnorm_jax(x, g):
    xf = x.astype(F32)
    return xf * lax.rsqrt(jnp.mean(xf * xf, axis=-1, keepdims=True) + NORM_EPS) * g.astype(F32)


def _rope_tables(positions, dim):
    inv_freq = ROPE_THETA ** (-jnp.arange(0, dim, 2, dtype=F32) / dim)
    ang = positions.astype(F32)[..., None] * inv_freq
    return jnp.cos(ang)[:, :, None, :], jnp.sin(ang)[:, :, None, :]


def _apply_rope(x, cos, sin):
    x1, x2 = jnp.split(x.astype(F32), 2, axis=-1)
    return jnp.concatenate([x1 * cos - x2 * sin, x2 * cos + x1 * sin], axis=-1)


NEG_BIG = -1e30
LANES = 128
INT_MIN = np.int32(-2 ** 31)


def _dsa_kernel(qi_ref, w_ref, q_ref, kit_ref, kt_ref, v_ref, o_ref,
                s_ref, wb_ref, m_ref, l_ref, acc_ref, *, tk, idx_heads, att_heads, topk, head_group):
    i = pl.program_id(1)
    n_tiles = ((i + 1) * QBLOCK + tk - 1) // tk
    nl = tk // LANES
    row = lax.broadcasted_iota(I32, (QBLOCK, tk), 0)
    col = lax.broadcasted_iota(I32, (QBLOCK, tk), 1)
    limit = (2 * i + 1 + (row >= CHUNK).astype(I32)) * CHUNK

    def admissible(t):
        return (t * tk + col) < limit

    def fold_lanes(x, op):
        out = x[:, :LANES]
        for j in range(1, nl):
            out = op(out, x[:, j * LANES:(j + 1) * LANES])
        return out

    w = w_ref[0, 0]
    for h in range(idx_heads):
        wb_ref[h] = jnp.broadcast_to(w[:, h:h + 1], (QBLOCK, LANES))

    def idx_body(t, carry):
        kit = kit_ref[0, t]
        acc = jnp.zeros((QBLOCK, tk), F32)
        for g in range(idx_heads // head_group):
            rows = head_group * QBLOCK
            rel = jnp.dot(qi_ref[0, 0, g * rows:(g + 1) * rows, :], kit, preferred_element_type=F32)
            for hh in range(head_group):
                h = g * head_group + hh
                relu = jnp.maximum(rel[hh * QBLOCK:(hh + 1) * QBLOCK], 0.0)
                acc = acc + jnp.concatenate([wb_ref[h]] * nl, axis=1) * relu
        score = jnp.where(admissible(t), acc, -jnp.inf)
        bits = lax.bitcast_convert_type(score, I32)
        s_ref[t] = bits ^ ((bits >> 31) & np.int32(0x7FFFFFFF))
        return carry

    lax.fori_loop(0, n_tiles, idx_body, 0)

    def bit_body(it, ans):
        cand = ans | lax.shift_left(np.int32(1), 31 - it)
        cand_s = cand ^ INT_MIN

        def cnt_body(t, cnt):
            return cnt + fold_lanes((s_ref[t] >= cand_s).astype(I32), jnp.add)

        cnt = lax.fori_loop(0, n_tiles, cnt_body, jnp.zeros((QBLOCK, LANES), I32))
        total = jnp.sum(cnt, axis=1, keepdims=True)
        return jnp.where(total >= topk, cand, ans)

    ans = lax.fori_loop(0, 32, bit_body, jnp.zeros((QBLOCK, 1), I32))
    thr = ans ^ INT_MIN

    def masked_logits(t):
        sel = jnp.logical_and(s_ref[t] >= thr, admissible(t))
        bias = jnp.where(sel, 0.0, NEG_BIG)
        logits = jnp.dot(q_ref[0, 0], kt_ref[0, t], preferred_element_type=F32)
        return logits + jnp.concatenate([bias] * att_heads, axis=0)

    m_ref[...] = jnp.full_like(m_ref, NEG_BIG)

    def max_body(t, carry):
        m_ref[...] = jnp.maximum(m_ref[...], fold_lanes(masked_logits(t), jnp.maximum))
        return carry

    lax.fori_loop(0, n_tiles, max_body, 0)
    m_ref[...] = jnp.broadcast_to(jnp.max(m_ref[...], axis=1, keepdims=True), m_ref.shape)

    l_ref[...] = jnp.zeros_like(l_ref)
    acc_ref[...] = jnp.zeros_like(acc_ref)

    def att_body(t, carry):
        p = jnp.exp(masked_logits(t) - jnp.concatenate([m_ref[...]] * nl, axis=1))
        l_ref[...] += fold_lanes(p, jnp.add)
        acc_ref[...] += jnp.dot(p.astype(BF16), v_ref[0, t], preferred_element_type=F32)
        return carry

    lax.fori_loop(0, n_tiles, att_body, 0)
    out = acc_ref[...] / jnp.sum(l_ref[...], axis=1, keepdims=True)
    d = out.shape[1]
    for h in range(att_heads):
        o_ref[0, :, h * d:(h + 1) * d] = out[h * QBLOCK:(h + 1) * QBLOCK, :]


def dsa_pallas(q, q_idx, w_idx, k, v, k_idx, topk, tk=512, head_group=8):
    bsz, seq, ha, d = q.shape
    hi, di = q_idx.shape[2:]
    nb = seq // QBLOCK
    nt = seq // tk
    qb = jnp.transpose(q.astype(BF16).reshape(bsz, nb, QBLOCK, ha, d), (0, 1, 3, 2, 4)).reshape(bsz, nb, ha * QBLOCK, d)
    qib = jnp.transpose(q_idx.astype(BF16).reshape(bsz, nb, QBLOCK, hi, di), (0, 1, 3, 2, 4)).reshape(bsz, nb, hi * QBLOCK, di)
    wb = w_idx.astype(F32).reshape(bsz, nb, QBLOCK, hi)
    kit = jnp.transpose(k_idx.astype(BF16).reshape(bsz, nt, tk, di), (0, 1, 3, 2))
    kt = jnp.transpose(k.astype(BF16).reshape(bsz, nt, tk, d), (0, 1, 3, 2))
    vb = v.astype(BF16).reshape(bsz, nt, tk, d)
    return pl.pallas_call(
        functools.partial(_dsa_kernel, tk=tk, idx_heads=hi, att_heads=ha, topk=topk, head_group=head_group),
        grid=(bsz, nb),
        in_specs=[
            pl.BlockSpec((1, 1, hi * QBLOCK, di), lambda b, i: (b, i, 0, 0)),
            pl.BlockSpec((1, 1, QBLOCK, hi), lambda b, i: (b, i, 0, 0)),
            pl.BlockSpec((1, 1, ha * QBLOCK, d), lambda b, i: (b, i, 0, 0)),
            pl.BlockSpec((1, nt, di, tk), lambda b, i: (b, 0, 0, 0)),
            pl.BlockSpec((1, nt, d, tk), lambda b, i: (b, 0, 0, 0)),
            pl.BlockSpec((1, nt, tk, d), lambda b, i: (b, 0, 0, 0)),
        ],
        out_specs=pl.BlockSpec((1, QBLOCK, ha * d), lambda b, i: (b, i, 0)),
        out_shape=jax.ShapeDtypeStruct((bsz, seq, ha * d), F32),
        scratch_shapes=[
            pltpu.VMEM((nt, QBLOCK, tk), I32),
            pltpu.VMEM((hi, QBLOCK, LANES), F32),
            pltpu.VMEM((ha * QBLOCK, LANES), F32),
            pltpu.VMEM((ha * QBLOCK, LANES), F32),
            pltpu.VMEM((ha * QBLOCK, d), F32),
        ],
        compiler_params=pltpu.CompilerParams(
            dimension_semantics=("parallel", "arbitrary"),
            vmem_limit_bytes=VMEM_LIMIT_BYTES),
        name="dsa",
    )(qib, wb, qb, kit, kt, vb)


def _bdot(a, b):
    return jnp.dot(a.astype(BF16), b.astype(BF16), preferred_element_type=F32)


def _bdot_nt(a, b):
    return lax.dot_general(a.astype(BF16), b.astype(BF16), (((1,), (1,)), ((), ())), preferred_element_type=F32)


def _bdot_tn(a, b):
    return lax.dot_general(a.astype(BF16), b.astype(BF16), (((0,), (0,)), ((), ())), preferred_element_type=F32)


def _wkv7_kernel(r_ref, k_ref, v_ref, a_ref, b_ref, lw_ref, y_ref, z_ref, p_ref, *, heads, clen):
    c = pl.program_id(2)
    row = lax.broadcasted_iota(I32, (clen, clen), 0)
    col = lax.broadcasted_iota(I32, (clen, clen), 1)
    strict = row > col
    incl = row >= col
    tri = incl.astype(F32)
    eye = (row == col).astype(F32)

    @pl.when(c == 0)
    def _():
        z_ref[...] = jnp.zeros_like(z_ref)
        p_ref[...] = jnp.ones_like(p_ref)

    hs = range(heads)
    lw = [lw_ref[0, h] for h in hs]
    cum = [jnp.dot(tri, lw[h], preferred_element_type=F32, precision=HI) for h in hs]
    ecum = [jnp.exp(cum[h]) for h in hs]
    encum = [jnp.exp(-cum[h]) for h in hs]
    rt = [r_ref[0, h] * ecum[h] for h in hs]
    kt = [k_ref[0, h] * encum[h] for h in hs]
    bt = [b_ref[0, h] * encum[h] for h in hs]
    at = [a_ref[0, h] * jnp.exp(cum[h] - lw[h]) for h in hs]
    v = [v_ref[0, h] for h in hs]
    x = [jnp.concatenate([at[h], rt[h]], axis=0) for h in hs]
    aab = [_bdot_nt(x[h], bt[h]) for h in hs]
    aak = [_bdot_nt(x[h], kt[h]) for h in hs]
    a_ab = [jnp.where(strict, aab[h][:clen], 0.0) for h in hs]
    a_ak = [jnp.where(strict, aak[h][:clen], 0.0) for h in hs]
    a_rb = [jnp.where(incl, aab[h][clen:], 0.0) for h in hs]
    a_rk = [jnp.where(incl, aak[h][clen:], 0.0) for h in hs]
    xp = a_ab
    t = [eye + a_ab[h] for h in hs]
    n = 2
    while n < clen:
        xp = [_bdot(xp[h], xp[h]) for h in hs]
        t = [t[h] + _bdot(t[h], xp[h]) for h in hs]
        n *= 2
    av = [_bdot(a_ak[h], v[h]) for h in hs]
    uv = [_bdot(t[h], av[h]) for h in hs]
    at2 = [_bdot(t[h], at[h]) for h in hs]
    rp = [rt[h] + _bdot(a_rb[h], at2[h]) for h in hs]
    yv = [_bdot(a_rb[h], uv[h]) + _bdot(a_rk[h], v[h]) for h in hs]
    mp = [_bdot_tn(bt[h], at2[h]) for h in hs]
    gp = [_bdot_tn(jnp.concatenate([bt[h], kt[h]], axis=0), jnp.concatenate([uv[h], v[h]], axis=0)) for h in hs]
    lhs = [jnp.concatenate([rp[h], eye + mp[h]], axis=0) * p_ref[h] for h in hs]
    out = [_bdot(lhs[h], z_ref[h]) for h in hs]
    for h in hs:
        y_ref[0, h] = out[h][:clen] + yv[h]
        z_ref[h] = out[h][clen:] + gp[h]
        p_ref[h] = ecum[h][clen - 1:clen, :]


def wkv7_pallas(r, k, v, a, b, lw, heads_per_step=8, clen=CHUNK):
    bsz, nh, seq, hd = r.shape
    assert hd == clen
    g = min(heads_per_step, nh)
    spec = pl.BlockSpec((1, g, clen, hd), lambda bi, hi, ci: (bi, hi, ci, 0))
    return pl.pallas_call(
        functools.partial(_wkv7_kernel, heads=g, clen=clen),
        grid=(bsz, nh // g, seq // clen),
        in_specs=[spec] * 6,
        out_specs=spec,
        out_shape=jax.ShapeDtypeStruct((bsz, nh, seq, hd), F32),
        scratch_shapes=[pltpu.VMEM((g, hd, hd), F32), pltpu.VMEM((g, 1, hd), F32)],
        compiler_params=pltpu.CompilerParams(dimension_semantics=("parallel", "parallel", "arbitrary")),
        name="wkv7",
    )(r, k, v, a, b, lw)


def _mm3(x, w, **kw):
    bsz, seq, kdim = x.shape
    kp = _round_up(kdim, 128)
    n = w.shape[1]
    np_ = _round_up(n, 128)
    xb = _pad_axis(x.reshape(bsz * seq, kdim).astype(BF16), 1, kp)
    wb = _pad_axis(_pad_axis(w.astype(BF16), 0, kp), 1, np_)
    tn = kw.pop("tn", 512)
    while np_ % tn:
        tn //= 2
    out = matmul_pallas(xb, wb, tn=tn, **kw)
    return out[:, :n].reshape(bsz, seq, n)


def _rwkv7_time_mix(p, shift_mu, w_up, w0, a_up, a0, g_up, k_k, k_a, r_k, gn_w, gn_b):
    bsz, seq, _ = p.shape
    p_prev = jnp.pad(p[:, :-1], ((0, 0), (1, 0), (0, 0)))
    p = p + (p_prev - p) * shift_mu
    r, k, v, xw, xa, xg = jnp.split(
        p, [B_WIDTH, 2 * B_WIDTH, 3 * B_WIDTH, 3 * B_WIDTH + B_DECAY_RANK,
            3 * B_WIDTH + B_DECAY_RANK + B_ICLR_RANK], axis=-1)
    w_log = -jax.nn.softplus(-(w0 + _mm3(jnp.tanh(xw), w_up))) - 0.5
    a = jax.nn.sigmoid(a0 + _mm3(xa, a_up))
    g = _mm3(jax.nn.sigmoid(xg), g_up)
    heads = lambda t: t.astype(F32).reshape(bsz, seq, B_HEADS, B_HEAD_DIM)
    r, k, v, a = heads(r), heads(k), heads(v), heads(a)
    log_decay = heads(-jnp.exp(w_log))
    kk = k * k_k.astype(F32).reshape(B_HEADS, B_HEAD_DIM)
    kk = kk * lax.rsqrt(jnp.maximum(jnp.sum(kk * kk, axis=-1, keepdims=True), 1e-24))
    k = k * (1.0 + (a - 1.0) * k_a.astype(F32).reshape(B_HEADS, B_HEAD_DIM))
    tr = lambda t: jnp.transpose(t, (0, 2, 1, 3))
    y = tr(wkv7_pallas(tr(r), tr(k), tr(v), tr(-kk), tr(kk * a), tr(log_decay)))
    mean = jnp.mean(y, axis=-1, keepdims=True)
    var = jnp.mean(jnp.square(y - mean), axis=-1, keepdims=True)
    y = ((y - mean) * lax.rsqrt(var + B_GN_EPS)).reshape(bsz, seq, B_WIDTH) * gn_w.astype(F32) + gn_b.astype(F32)
    bonus = jnp.sum(r * k * r_k.astype(F32), axis=-1, keepdims=True) * v
    y = y + bonus.reshape(bsz, seq, B_WIDTH)
    return y * g


def _ssd_chunked_scan(x, dt, a, b_in, c_out):
    bsz, seq, ng, hg, hp = x.shape
    ns = b_in.shape[-1]
    n_chunks = seq // CHUNK
    causal = jnp.tril(jnp.ones((CHUNK, CHUNK), dtype=bool))

    def to_chunks(t):
        return jnp.moveaxis(t.astype(F32).reshape(bsz, n_chunks, CHUNK, *t.shape[2:]), 1, 0)

    def step(state, inp):
        xc, dtc, bc, cc = inp
        cum = jnp.cumsum(dtc * a, axis=1)
        seg = cum[:, :, None] - cum[:, None, :]
        decay = jnp.exp(jnp.where(causal[None, :, :, None, None], seg, -jnp.inf))
        cb = jnp.einsum('btgn,bsgn->btsg', cc, bc)
        wts = cb[..., None] * decay * dtc[:, None]
        y = jnp.einsum('btsgh,bsghp->btghp', wts, xc)
        y = y + jnp.einsum('btgn,bghpn->btghp', cc, state) * jnp.exp(cum)[..., None]
        to_end = jnp.exp(cum[:, -1:] - cum) * dtc
        state = state * jnp.exp(cum[:, -1])[..., None, None] + jnp.einsum('bsgn,bsgh,bsghp->bghpn', bc, to_end, xc)
        return state, y

    state0 = jnp.zeros((bsz, ng, hg, hp, ns), F32)
    _, ys = lax.scan(step, state0, (to_chunks(x), to_chunks(dt), to_chunks(b_in), to_chunks(c_out)))
    return jnp.moveaxis(ys, 0, 1).reshape(bsz, seq, ng, hg, hp)


def _dsa_rwkv_layer(x, positions, norm_mix, w_in, q_norm, w_uq, w_qi, kv_norm, w_uk, w_uv,
                    shift_mu, w_up, w0, a_up, a0, g_up, k_k, k_a, r_k, gn_w, gn_b, w_out,
                    norm_ffn, ffn_gate, ffn_up, ffn_down):
    bsz, seq, d = x.shape
    m = bsz * seq
    x2 = x.reshape(m, d)
    cos_a, sin_a = _rope_tables(positions, A_HEAD_DIM)
    cos_i, sin_i = _rope_tables(positions, IDX_DIM)

    mix_in = w_in.shape[1]
    w_in_b = _pad_axis(w_in.astype(BF16), 1, _round_up(mix_in, 512))
    proj = matmul_pallas(rmsnorm_pallas(x2, norm_mix), w_in_b, name="l0_in")[:, :mix_in]
    proj = proj.reshape(bsz, seq, mix_in)
    c_q, c_kv, k_idx, w_idx, p_b = jnp.split(
        proj, [A_Q_RANK, A_Q_RANK + A_KV_RANK, A_Q_RANK + A_KV_RANK + IDX_DIM, A_IN], axis=-1)
    c_q = _rmsnorm_jax(c_q, q_norm)
    c_kv = _rmsnorm_jax(c_kv, kv_norm)
    q = _apply_rope(_mm3(c_q, w_uq).reshape(bsz, seq, A_HEADS, A_HEAD_DIM), cos_a, sin_a)
    q_idx = _apply_rope(_mm3(c_q, w_qi).reshape(bsz, seq, IDX_HEADS, IDX_DIM), cos_i, sin_i)
    kv = _mm3(c_kv, jnp.concatenate([w_uk, w_uv], axis=1))
    k = _apply_rope(kv[:, :, None, :A_HEAD_DIM], cos_a, sin_a)[:, :, 0]
    v = kv[:, :, A_HEAD_DIM:]
    k_idx = _apply_rope(k_idx[:, :, None, :], cos_i, sin_i)[:, :, 0]
    w_idx = w_idx * (IDX_HEADS * IDX_DIM) ** -0.5
    y_a = dsa_pallas(q * A_HEAD_DIM ** -0.5, q_idx, w_idx, k, v, k_idx, topk=min(IDX_TOPK_MAX, seq // 4))
    y_b = _rwkv7_time_mix(p_b, shift_mu, w_up, w0, a_up, a0, g_up, k_k, k_a, r_k, gn_w, gn_b)
    y = jnp.concatenate([y_a, y_b], axis=-1).reshape(m, -1).astype(BF16)
    x2 = matmul_pallas(y, w_out.astype(BF16), res=x2, name="l0_out")

    ffn = ffn_gate.shape[1]
    ffn_p = _round_up(ffn, 512)
    h = rmsnorm_pallas(x2, norm_ffn)
    wg = _pad_axis(ffn_gate.astype(BF16), 1, ffn_p)[None]
    wu = _pad_axis(ffn_up.astype(BF16), 1, ffn_p)[None]
    wd = _pad_axis(ffn_down.astype(BF16), 0, ffn_p)
    hh = glu_pallas(h, wg, wu, name="l0_glu")
    x2 = matmul_pallas(hh, wd, res=x2, tm=1024, tn=1024, tk=ffn_p // 4, name="l0_down")
    return x2.reshape(bsz, seq, d)


def _mamba2_mixer(h2, bsz, seq, w_in, conv_w, conv_b, dt_bias, a_log, d_skip, gate_norm, w_out):
    c_inner = w_out.shape[0]
    c_heads = c_inner // C_HEAD_DIM
    hpg = c_heads // C_GROUPS
    c_conv_dim = c_inner + 2 * C_GROUPS * C_STATE
    c_in = w_in.shape[1]
    w_in_b = _pad_axis(w_in.astype(BF16), 1, _round_up(c_in, 512))
    proj = matmul_pallas(h2, w_in_b, name="l1_in")[:, :c_in].reshape(bsz, seq, c_in)
    z, xbc, dt = jnp.split(proj, [c_inner, c_inner + c_conv_dim], axis=-1)
    xbc = lax.conv_general_dilated(xbc, conv_w[:, None, :], window_strides=(1,), padding=[(C_CONV - 1, 0)],
                                   dimension_numbers=('NWC', 'WIO', 'NWC'), feature_group_count=c_conv_dim,
                                   precision=lax.Precision.HIGHEST)
    xbc = jax.nn.silu(xbc + conv_b)
    xs, b_in, c_out = jnp.split(xbc, [c_inner, c_inner + C_GROUPS * C_STATE], axis=-1)
    xs = xs.reshape(bsz, seq, C_GROUPS, hpg, C_HEAD_DIM)
    b_in = b_in.reshape(bsz, seq, C_GROUPS, C_STATE)
    c_out = c_out.reshape(bsz, seq, C_GROUPS, C_STATE)
    dt = jax.nn.softplus(dt.astype(F32) + dt_bias.astype(F32)).reshape(bsz, seq, C_GROUPS, hpg)
    a = -jnp.exp(a_log.astype(F32)).reshape(C_GROUPS, hpg)
    y = _ssd_chunked_scan(xs, dt, a, b_in, c_out)
    y = y + d_skip.astype(F32).reshape(C_GROUPS, hpg)[..., None] * xs.astype(F32)
    y = y.reshape(bsz, seq, c_inner) * jax.nn.silu(z.astype(F32))
    yg = y.reshape(bsz, seq, C_GROUPS, c_inner // C_GROUPS)
    yg = yg * lax.rsqrt(jnp.mean(yg * yg, axis=-1, keepdims=True) + NORM_EPS)
    y = yg.reshape(bsz, seq, c_inner) * gate_norm.astype(F32)
    return y.reshape(bsz * seq, c_inner).astype(BF16)


def _ssd_moe_layer(x, norm_mix, w_in, conv_w, conv_b, dt_bias, a_log, d_skip, gate_norm, w_out,
                   norm_ffn, router, exp_gate, exp_up, exp_down):
    bsz, seq, d = x.shape
    m = bsz * seq
    x2 = x.reshape(m, d)
    y = _mamba2_mixer(rmsnorm_pallas(x2, norm_mix), bsz, seq, w_in, conv_w, conv_b, dt_bias, a_log, d_skip,
                      gate_norm, w_out)
    x2 = matmul_pallas(y, w_out.astype(BF16), res=x2, tm=1024, tn=1024, tk=2048, name="l1_out")

    h32 = rmsnorm_pallas(x2, norm_ffn, out_dtype=F32)
    logits = jnp.dot(h32, router, precision=lax.Precision.HIGHEST)
    top_vals, top_idx = lax.top_k(logits, TOP_K_EXPERTS)
    top_w = jax.nn.softmax(top_vals, axis=-1)
    gates = jnp.sum(jax.nn.one_hot(top_idx, N_EXPERTS, dtype=F32) * top_w[..., None], axis=-2)
    h = h32.astype(BF16)
    hh = glu_pallas(h, exp_gate.astype(BF16), exp_up.astype(BF16), gates=gates, name="l1_moe_glu")
    wd = exp_down.astype(BF16).reshape(-1, d)
    x2 = matmul_pallas(hh, wd, res=x2, tm=1024, tn=1024, tk=2048, name="l1_moe_down")
    return x2.reshape(bsz, seq, d)


def kernel(x, positions, l0_norm_mix, l0_w_in, l0_q_norm, l0_w_uq, l0_w_qi, l0_kv_norm, l0_w_uk, l0_w_uv,
           l0_shift_mu, l0_w_up, l0_w0, l0_a_up, l0_a0, l0_g_up, l0_k_k, l0_k_a, l0_r_k, l0_gn_w, l0_gn_b,
           l0_w_out, l0_norm_ffn, l0_ffn_gate, l0_ffn_up, l0_ffn_down,
           l1_norm_mix, l1_w_in, l1_conv_w, l1_conv_b, l1_dt_bias, l1_a_log, l1_d_skip, l1_gate_norm, l1_w_out,
           l1_norm_ffn, l1_router, l1_exp_gate, l1_exp_up, l1_exp_down, final_norm):
    x = _dsa_rwkv_layer(x, positions, l0_norm_mix, l0_w_in, l0_q_norm, l0_w_uq, l0_w_qi, l0_kv_norm, l0_w_uk,
                        l0_w_uv, l0_shift_mu, l0_w_up, l0_w0, l0_a_up, l0_a0, l0_g_up, l0_k_k, l0_k_a, l0_r_k,
                        l0_gn_w, l0_gn_b, l0_w_out, l0_norm_ffn, l0_ffn_gate, l0_ffn_up, l0_ffn_down)
    x = _ssd_moe_layer(x, l1_norm_mix, l1_w_in, l1_conv_w, l1_conv_b, l1_dt_bias, l1_a_log, l1_d_skip,
                       l1_gate_norm, l1_w_out, l1_norm_ffn, l1_router, l1_exp_gate, l1_exp_up, l1_exp_down)
    bsz, seq, d = x.shape
    return rmsnorm_pallas(x.reshape(bsz * seq, d), final_norm, out_dtype=x.dtype).reshape(bsz, seq, d)
```

```python
import functools

import numpy as np
import jax
import jax.numpy as jnp
from jax import lax
from jax.experimental import pallas as pl
from jax.experimental.pallas import tpu as pltpu

F32 = jnp.float32
BF16 = jnp.bfloat16
I32 = jnp.int32
HI = lax.Precision.HIGHEST

CHUNK = 64
QBLOCK = 128
ROPE_THETA = 10000.0
NORM_EPS = 1e-6

A_HEADS = 16
A_HEAD_DIM = 128
A_Q_RANK = 1024
A_KV_RANK = 512
IDX_HEADS = 32
IDX_DIM = 128
IDX_TOPK_MAX = 256

B_HEADS = 32
B_HEAD_DIM = 64
B_WIDTH = B_HEADS * B_HEAD_DIM
B_DECAY_RANK = 128
B_ICLR_RANK = 128
B_GATE_RANK = 480
B_GN_EPS = 64e-5
A_IN = A_Q_RANK + A_KV_RANK + IDX_DIM + IDX_HEADS

C_HEAD_DIM = 64
C_GROUPS = 8
C_STATE = 128
C_CONV = 4

TOP_K_EXPERTS = 2

LANES = 128
VMEM_LIMIT_BYTES = 56 * 1024 * 1024


def _round_up(n, m):
    return (n + m - 1) // m * m


def _pad_axis(a, axis, size):
    if a.shape[axis] == size:
        return a
    pads = [(0, 0)] * a.ndim
    pads[axis] = (0, size - a.shape[axis])
    return jnp.pad(a, pads)


def _rmsnorm_kernel(x_ref, g_ref, o_ref):
    x = x_ref[...].astype(F32)
    ms = jnp.mean(x * x, axis=-1, keepdims=True)
    o_ref[...] = (x * lax.rsqrt(ms + NORM_EPS) * g_ref[...]).astype(o_ref.dtype)


def rmsnorm_pallas(x, g, out_dtype=BF16, tm=256):
    m, d = x.shape
    tm = min(tm, m)
    return pl.pallas_call(
        _rmsnorm_kernel,
        grid=(m // tm,),
        in_specs=[pl.BlockSpec((tm, d), lambda i: (i, 0)),
                  pl.BlockSpec((1, d), lambda i: (0, 0))],
        out_specs=pl.BlockSpec((tm, d), lambda i: (i, 0)),
        out_shape=jax.ShapeDtypeStruct((m, d), out_dtype),
        compiler_params=pltpu.CompilerParams(dimension_semantics=("parallel",)),
        name="rmsnorm",
    )(x, g.reshape(1, d).astype(F32))


def _mm_kernel(x_ref, w_ref, *rest, nk, has_res):
    if has_res:
        res_ref, o_ref, acc_ref = rest
    else:
        o_ref, acc_ref = rest
        res_ref = None
    p = jnp.dot(x_ref[...], w_ref[...], preferred_element_type=F32)

    def finish(acc):
        if has_res:
            acc = acc + res_ref[...].astype(F32)
        o_ref[...] = acc.astype(o_ref.dtype)

    if nk == 1:
        finish(p)
        return
    k = pl.program_id(2)

    @pl.when(k == 0)
    def _():
        acc_ref[...] = p

    @pl.when(jnp.logical_and(k > 0, k < nk - 1))
    def _():
        acc_ref[...] += p

    @pl.when(k == nk - 1)
    def _():
        finish(acc_ref[...] + p)


def matmul_pallas(x, w, res=None, out_dtype=F32, tm=1024, tn=512, tk=None, name="matmul"):
    m, kdim = x.shape
    k2, n = w.shape
    assert kdim == k2
    tm = min(tm, m)
    tn = min(tn, n)
    tk = kdim if tk is None else min(tk, kdim)
    assert m % tm == 0 and n % tn == 0 and kdim % tk == 0, (x.shape, w.shape, tm, tn, tk)
    nk = kdim // tk
    in_specs = [pl.BlockSpec((tm, tk), lambda i, j, k: (i, k)),
                pl.BlockSpec((tk, tn), lambda i, j, k: (k, j))]
    args = [x, w]
    if res is not None:
        in_specs.append(pl.BlockSpec((tm, tn), lambda i, j, k: (i, j)))
        args.append(res)
    acc_shape = (tm, tn) if nk > 1 else (8, 128)
    return pl.pallas_call(
        functools.partial(_mm_kernel, nk=nk, has_res=res is not None),
        grid=(m // tm, n // tn, nk),
        in_specs=in_specs,
        out_specs=pl.BlockSpec((tm, tn), lambda i, j, k: (i, j)),
        out_shape=jax.ShapeDtypeStruct((m, n), out_dtype),
        scratch_shapes=[pltpu.VMEM(acc_shape, F32)],
        compiler_params=pltpu.CompilerParams(
            dimension_semantics=("parallel", "parallel", "arbitrary"),
            vmem_limit_bytes=VMEM_LIMIT_BYTES),
        name=name,
    )(*args)


def _gglu_kernel(te_ref, nu_ref, x_ref, wg_ref, wu_ref, o_ref):
    t = pl.program_id(0)

    @pl.when(t < nu_ref[0])
    def _():
        x = x_ref[...]
        g = jnp.dot(x, wg_ref[...], preferred_element_type=F32)
        u = jnp.dot(x, wu_ref[...], preferred_element_type=F32)
        o_ref[...] = (g * jax.nn.sigmoid(g) * u).astype(o_ref.dtype)

    @pl.when(t >= nu_ref[0])
    def _():
        o_ref[...] = jnp.zeros_like(o_ref)


def grouped_glu_pallas(xs, wg, wu, tile_expert, n_used, tm, tn=512, name="glu"):
    r, kdim = xs.shape
    n = wg.shape[2]
    assert r % tm == 0 and n % tn == 0
    grid_spec = pltpu.PrefetchScalarGridSpec(
        num_scalar_prefetch=2,
        grid=(r // tm, n // tn),
        in_specs=[pl.BlockSpec((tm, kdim), lambda t, j, te, nu: (t, 0)),
                  pl.BlockSpec((None, kdim, tn), lambda t, j, te, nu: (te[t], 0, j)),
                  pl.BlockSpec((None, kdim, tn), lambda t, j, te, nu: (te[t], 0, j))],
        out_specs=pl.BlockSpec((tm, tn), lambda t, j, te, nu: (t, j)),
    )
    return pl.pallas_call(
        _gglu_kernel,
        grid_spec=grid_spec,
        out_shape=jax.ShapeDtypeStruct((r, n), BF16),
        compiler_params=pltpu.CompilerParams(
            dimension_semantics=("arbitrary", "arbitrary"), vmem_limit_bytes=VMEM_LIMIT_BYTES),
        name=name,
    )(tile_expert, n_used, xs, wg, wu)


def _gdown_kernel(te_ref, nu_ref, x_ref, w_ref, o_ref):
    t = pl.program_id(0)

    @pl.when(t < nu_ref[0])
    def _():
        o_ref[...] = jnp.dot(x_ref[...], w_ref[...], preferred_element_type=F32).astype(o_ref.dtype)

    @pl.when(t >= nu_ref[0])
    def _():
        o_ref[...] = jnp.zeros_like(o_ref)


def grouped_down_pallas(hs, wd, tile_expert, n_used, tm, tn=512, name="down"):
    r, kdim = hs.shape
    n = wd.shape[2]
    assert r % tm == 0 and n % tn == 0
    grid_spec = pltpu.PrefetchScalarGridSpec(
        num_scalar_prefetch=2,
        grid=(r // tm, n // tn),
        in_specs=[pl.BlockSpec((tm, kdim), lambda t, j, te, nu: (t, 0)),
                  pl.BlockSpec((None, kdim, tn), lambda t, j, te, nu: (te[t], 0, j))],
        out_specs=pl.BlockSpec((tm, tn), lambda t, j, te, nu: (t, j)),
    )
    return pl.pallas_call(
        _gdown_kernel,
        grid_spec=grid_spec,
        out_shape=jax.ShapeDtypeStruct((r, n), F32),
        compiler_params=pltpu.CompilerParams(
            dimension_semantics=("arbitrary", "arbitrary"), vmem_limit_bytes=VMEM_LIMIT_BYTES),
        name=name,
    )(tile_expert, n_used, hs, wd)


def _router_kernel(x_ref, g_ref, wr_ref, h_ref, r_ref, *, n_experts):
    x = x_ref[...]
    h = x * lax.rsqrt(jnp.mean(x * x, axis=-1, keepdims=True) + NORM_EPS) * g_ref[...]
    h_ref[...] = h.astype(h_ref.dtype)
    logits = jnp.dot(h, wr_ref[...], preferred_element_type=F32, precision=HI)
    lane = lax.broadcasted_iota(I32, logits.shape, 1)
    logits = jnp.where(lane < n_experts, logits, -jnp.inf)
    m1 = jnp.max(logits, axis=1, keepdims=True)
    i1 = jnp.min(jnp.where(logits == m1, lane, LANES), axis=1, keepdims=True)
    rest = jnp.where(lane == i1, -jnp.inf, logits)
    m2 = jnp.max(rest, axis=1, keepdims=True)
    i2 = jnp.min(jnp.where(rest == m2, lane, LANES), axis=1, keepdims=True)
    e2 = jnp.exp(m2 - m1)
    w1 = 1.0 / (1.0 + e2)
    w2 = e2 / (1.0 + e2)
    out = jnp.where(lane == 0, i1.astype(F32), 0.0)
    out = jnp.where(lane == 1, i2.astype(F32), out)
    out = jnp.where(lane == 2, w1, out)
    out = jnp.where(lane == 3, w2, out)
    r_ref[...] = out


def router_pallas(x, g, router, tm=256):
    m, d = x.shape
    ne = router.shape[1]
    wr = jnp.pad(router.astype(F32), ((0, 0), (0, LANES - ne)))
    return pl.pallas_call(
        functools.partial(_router_kernel, n_experts=ne),
        grid=(m // tm,),
        in_specs=[pl.BlockSpec((tm, d), lambda i: (i, 0)),
                  pl.BlockSpec((1, d), lambda i: (0, 0)),
                  pl.BlockSpec((d, LANES), lambda i: (0, 0))],
        out_specs=[pl.BlockSpec((tm, d), lambda i: (i, 0)),
                   pl.BlockSpec((tm, LANES), lambda i: (i, 0))],
        out_shape=[jax.ShapeDtypeStruct((m, d), BF16), jax.ShapeDtypeStruct((m, LANES), F32)],
        compiler_params=pltpu.CompilerParams(dimension_semantics=("parallel",)),
        name="router",
    )(x, g.reshape(1, d).astype(F32), wr)


def moe_sparse(x2, norm_ffn, router, exp_gate, exp_up, exp_down, tm=512):
    m, d = x2.shape
    ne = router.shape[1]
    h, route = router_pallas(x2, norm_ffn, router)
    idx = route[:, :TOP_K_EXPERTS].astype(I32)
    wts = route[:, TOP_K_EXPERTS:2 * TOP_K_EXPERTS]
    flat_e = idx.reshape(-1)
    onehot = (flat_e[:, None] == jnp.arange(ne, dtype=I32)[None, :]).astype(I32)
    rank = jnp.sum((jnp.cumsum(onehot, axis=0) - 1) * onehot, axis=1)
    count = jnp.sum(onehot, axis=0)
    padded = (count + tm - 1) // tm * tm
    ends = jnp.cumsum(padded)
    pos = (ends - padded)[flat_e] + rank
    rows = m * TOP_K_EXPERTS + ne * tm
    n_tiles = rows // tm
    token_of_row = jnp.zeros((rows,), I32).at[pos].set(jnp.arange(m * TOP_K_EXPERTS, dtype=I32) // TOP_K_EXPERTS)
    tile_start = jnp.arange(n_tiles, dtype=I32)[:, None] * tm
    tile_expert = jnp.minimum(jnp.sum((tile_start >= ends[None, :]).astype(I32), axis=1), ne - 1).astype(I32)
    n_used = (ends[-1] // tm).astype(I32).reshape(1)
    xs = jnp.take(h, token_of_row, axis=0)
    hh = grouped_glu_pallas(xs, exp_gate.astype(BF16), exp_up.astype(BF16), tile_expert, n_used, tm, name="moe_glu")
    ys = grouped_down_pallas(hh, exp_down.astype(BF16), tile_expert, n_used, tm, name="moe_down")
    pos2 = pos.reshape(m, TOP_K_EXPERTS)
    y = jnp.take(ys, pos2[:, 0], axis=0) * wts[:, 0:1] + jnp.take(ys, pos2[:, 1], axis=0) * wts[:, 1:2]
    return x2 + y


def _rmsnorm_jax(x, g):
    xf = x.astype(F32)
    return xf * lax.rsqrt(jnp.mean(xf * xf, axis=-1, keepdims=True) + NORM_EPS) * g.astype(F32)


def _rope_tables(positions, dim):
    inv_freq = ROPE_THETA ** (-jnp.arange(0, dim, 2, dtype=F32) / dim)
    ang = positions.astype(F32)[..., None] * inv_freq
    return jnp.cos(ang)[:, :, None, :], jnp.sin(ang)[:, :, None, :]


def _apply_rope(x, cos, sin):
    x1, x2 = jnp.split(x.astype(F32), 2, axis=-1)
    return jnp.concatenate([x1 * cos - x2 * sin, x2 * cos + x1 * sin], axis=-1)


NEG_BIG = -1e30
INT_MIN = np.int32(-2 ** 31)


def _dsa_kernel(qi_ref, w_ref, q_ref, kit_ref, kt_ref, v_ref, o_ref,
                s_ref, wb_ref, m_ref, l_ref, acc_ref, *, tk, idx_heads, att_heads, topk, head_group):
    i = pl.program_id(1)
    n_tiles = ((i + 1) * QBLOCK + tk - 1) // tk
    nl = tk // LANES
    row = lax.broadcasted_iota(I32, (QBLOCK, tk), 0)
    col = lax.broadcasted_iota(I32, (QBLOCK, tk), 1)
    limit = (2 * i + 1 + (row >= CHUNK).astype(I32)) * CHUNK

    def admissible(t):
        return (t * tk + col) < limit

    def fold_lanes(x, op):
        out = x[:, :LANES]
        for j in range(1, nl):
            out = op(out, x[:, j * LANES:(j + 1) * LANES])
        return out

    w = w_ref[0, 0]
    for h in range(idx_heads):
        wb_ref[h] = jnp.broadcast_to(w[:, h:h + 1], (QBLOCK, LANES))

    def idx_body(t, carry):
        kit = kit_ref[0, t]
        acc = jnp.zeros((QBLOCK, tk), F32)
        for g in range(idx_heads // head_group):
            rows = head_group * QBLOCK
            rel = jnp.dot(qi_ref[0, 0, g * rows:(g + 1) * rows, :], kit, preferred_element_type=F32)
            for hh in range(head_group):
                h = g * head_group + hh
                relu = jnp.maximum(rel[hh * QBLOCK:(hh + 1) * QBLOCK], 0.0)
                acc = acc + jnp.concatenate([wb_ref[h]] * nl, axis=1) * relu
        score = jnp.where(admissible(t), acc, -jnp.inf)
        bits = lax.bitcast_convert_type(score, I32)
        s_ref[t] = bits ^ ((bits >> 31) & np.int32(0x7FFFFFFF))
        return carry

    lax.fori_loop(0, n_tiles, idx_body, 0)

    def bit_body(it, ans):
        cand = ans | lax.shift_left(np.int32(1), 31 - it)
        cand_s = cand ^ INT_MIN

        def cnt_body(t, cnt):
            return cnt + fold_lanes((s_ref[t] >= cand_s).astype(I32), jnp.add)

        cnt = lax.fori_loop(0, n_tiles, cnt_body, jnp.zeros((QBLOCK, LANES), I32))
        total = jnp.sum(cnt, axis=1, keepdims=True)
        return jnp.where(total >= topk, cand, ans)

    ans = lax.fori_loop(0, 32, bit_body, jnp.zeros((QBLOCK, 1), I32))
    thr = ans ^ INT_MIN

    def masked_logits(t):
        sel = jnp.logical_and(s_ref[t] >= thr, admissible(t))
        bias = jnp.where(sel, 0.0, NEG_BIG)
        logits = jnp.dot(q_ref[0, 0], kt_ref[0, t], preferred_element_type=F32)
        return logits + jnp.concatenate([bias] * att_heads, axis=0)

    m_ref[...] = jnp.full_like(m_ref, NEG_BIG)

    def max_body(t, carry):
        m_ref[...] = jnp.maximum(m_ref[...], fold_lanes(masked_logits(t), jnp.maximum))
        return carry

    lax.fori_loop(0, n_tiles, max_body, 0)
    m_ref[...] = jnp.broadcast_to(jnp.max(m_ref[...], axis=1, keepdims=True), m_ref.shape)

    l_ref[...] = jnp.zeros_like(l_ref)
    acc_ref[...] = jnp.zeros_like(acc_ref)

    def att_body(t, carry):
        p = jnp.exp(masked_logits(t) - jnp.concatenate([m_ref[...]] * nl, axis=1))
        l_ref[...] += fold_lanes(p, jnp.add)
        acc_ref[...] += jnp.dot(p.astype(BF16), v_ref[0, t], preferred_element_type=F32)
        return carry

    lax.fori_loop(0, n_tiles, att_body, 0)
    out = acc_ref[...] / jnp.sum(l_ref[...], axis=1, keepdims=True)
    d = out.shape[1]
    for h in range(att_heads):
        o_ref[0, :, h * d:(h + 1) * d] = out[h * QBLOCK:(h + 1) * QBLOCK, :]


def dsa_pallas(q, q_idx, w_idx, k, v, k_idx, topk, tk=512, head_group=8):
    bsz, seq, ha, d = q.shape
    hi, di = q_idx.shape[2:]
    nb = seq // QBLOCK
    nt = seq // tk
    qb = jnp.transpose(q.astype(BF16).reshape(bsz, nb, QBLOCK, ha, d), (0, 1, 3, 2, 4)).reshape(bsz, nb, ha * QBLOCK, d)
    qib = jnp.transpose(q_idx.astype(BF16).reshape(bsz, nb, QBLOCK, hi, di), (0, 1, 3, 2, 4)).reshape(bsz, nb, hi * QBLOCK, di)
    wb = w_idx.astype(F32).reshape(bsz, nb, QBLOCK, hi)
    kit = jnp.transpose(k_idx.astype(BF16).reshape(bsz, nt, tk, di), (0, 1, 3, 2))
    kt = jnp.transpose(k.astype(BF16).reshape(bsz, nt, tk, d), (0, 1, 3, 2))
    vb = v.astype(BF16).reshape(bsz, nt, tk, d)
    return pl.pallas_call(
        functools.partial(_dsa_kernel, tk=tk, idx_heads=hi, att_heads=ha, topk=topk, head_group=head_group),
        grid=(bsz, nb),
        in_specs=[
            pl.BlockSpec((1, 1, hi * QBLOCK, di), lambda b, i: (b, i, 0, 0)),
            pl.BlockSpec((1, 1, QBLOCK, hi), lambda b, i: (b, i, 0, 0)),
            pl.BlockSpec((1, 1, ha * QBLOCK, d), lambda b, i: (b, i, 0, 0)),
            pl.BlockSpec((1, nt, di, tk), lambda b, i: (b, 0, 0, 0)),
            pl.BlockSpec((1, nt, d, tk), lambda b, i: (b, 0, 0, 0)),
            pl.BlockSpec((1, nt, tk, d), lambda b, i: (b, 0, 0, 0)),
        ],
        out_specs=pl.BlockSpec((1, QBLOCK, ha * d), lambda b, i: (b, i, 0)),
        out_shape=jax.ShapeDtypeStruct((bsz, seq, ha * d), F32),
        scratch_shapes=[
            pltpu.VMEM((nt, QBLOCK, tk), I32),
            pltpu.VMEM((hi, QBLOCK, LANES), F32),
            pltpu.VMEM((ha * QBLOCK, LANES), F32),
            pltpu.VMEM((ha * QBLOCK, LANES), F32),
            pltpu.VMEM((ha * QBLOCK, d), F32),
        ],
        compiler_params=pltpu.CompilerParams(
            dimension_semantics=("parallel", "arbitrary"),
            vmem_limit_bytes=VMEM_LIMIT_BYTES),
        name="dsa",
    )(qib, wb, qb, kit, kt, vb)


def _bdot(a, b):
    return jnp.dot(a.astype(BF16), b.astype(BF16), preferred_element_type=F32)


def _bdot_nt(a, b):
    return lax.dot_general(a.astype(BF16), b.astype(BF16), (((1,), (1,)), ((), ())), preferred_element_type=F32)


def _bdot_tn(a, b):
    return lax.dot_general(a.astype(BF16), b.astype(BF16), (((0,), (0,)), ((), ())), preferred_element_type=F32)


def _wkv7_kernel(r_ref, k_ref, v_ref, a_ref, b_ref, lw_ref, y_ref, z_ref, p_ref, *, heads, clen):
    c = pl.program_id(2)
    row = lax.broadcasted_iota(I32, (clen, clen), 0)
    col = lax.broadcasted_iota(I32, (clen, clen), 1)
    strict = row > col
    incl = row >= col
    tri = incl.astype(F32)
    eye = (row == col).astype(F32)

    @pl.when(c == 0)
    def _():
        z_ref[...] = jnp.zeros_like(z_ref)
        p_ref[...] = jnp.ones_like(p_ref)

    hs = range(heads)
    lw = [lw_ref[0, h] for h in hs]
    cum = [jnp.dot(tri, lw[h], preferred_element_type=F32, precision=HI) for h in hs]
    ecum = [jnp.exp(cum[h]) for h in hs]
    encum = [jnp.exp(-cum[h]) for h in hs]
    rt = [r_ref[0, h] * ecum[h] for h in hs]
    kt = [k_ref[0, h] * encum[h] for h in hs]
    bt = [b_ref[0, h] * encum[h] for h in hs]
    at = [a_ref[0, h] * jnp.exp(cum[h] - lw[h]) for h in hs]
    v = [v_ref[0, h] for h in hs]
    x = [jnp.concatenate([at[h], rt[h]], axis=0) for h in hs]
    aab = [_bdot_nt(x[h], bt[h]) for h in hs]
    aak = [_bdot_nt(x[h], kt[h]) for h in hs]
    a_ab = [jnp.where(strict, aab[h][:clen], 0.0) for h in hs]
    a_ak = [jnp.where(strict, aak[h][:clen], 0.0) for h in hs]
    a_rb = [jnp.where(incl, aab[h][clen:], 0.0) for h in hs]
    a_rk = [jnp.where(incl, aak[h][clen:], 0.0) for h in hs]
    xp = a_ab
    t = [eye + a_ab[h] for h in hs]
    n = 2
    while n < clen:
        xp = [_bdot(xp[h], xp[h]) for h in hs]
        t = [t[h] + _bdot(t[h], xp[h]) for h in hs]
        n *= 2
    av = [_bdot(a_ak[h], v[h]) for h in hs]
    uv = [_bdot(t[h], av[h]) for h in hs]
    at2 = [_bdot(t[h], at[h]) for h in hs]
    rp = [rt[h] + _bdot(a_rb[h], at2[h]) for h in hs]
    yv = [_bdot(a_rb[h], uv[h]) + _bdot(a_rk[h], v[h]) for h in hs]
    mp = [_bdot_tn(bt[h], at2[h]) for h in hs]
    gp = [_bdot_tn(jnp.concatenate([bt[h], kt[h]], axis=0), jnp.concatenate([uv[h], v[h]], axis=0)) for h in hs]
    lhs = [jnp.concatenate([rp[h], eye + mp[h]], axis=0) * p_ref[h] for h in hs]
    out = [_bdot(lhs[h], z_ref[h]) for h in hs]
    for h in hs:
        y_ref[0, h] = out[h][:clen] + yv[h]
        z_ref[h] = out[h][clen:] + gp[h]
        p_ref[h] = ecum[h][clen - 1:clen, :]


def wkv7_pallas(r, k, v, a, b, lw, heads_per_step=8, clen=CHUNK):
    bsz, nh, seq, hd = r.shape
    assert hd == clen
    g = min(heads_per_step, nh)
    spec = pl.BlockSpec((1, g, clen, hd), lambda bi, hi, ci: (bi, hi, ci, 0))
    return pl.pallas_call(
        functools.partial(_wkv7_kernel, heads=g, clen=clen),
        grid=(bsz, nh // g, seq // clen),
        in_specs=[spec] * 6,
        out_specs=spec,
        out_shape=jax.ShapeDtypeStruct((bsz, nh, seq, hd), F32),
        scratch_shapes=[pltpu.VMEM((g, hd, hd), F32), pltpu.VMEM((g, 1, hd), F32)],
        compiler_params=pltpu.CompilerParams(dimension_semantics=("parallel", "parallel", "arbitrary")),
        name="wkv7",
    )(r, k, v, a, b, lw)


def _mm3(x, w, **kw):
    bsz, seq, kdim = x.shape
    kp = _round_up(kdim, 128)
    n = w.shape[1]
    np_ = _round_up(n, 128)
    xb = _pad_axis(x.reshape(bsz * seq, kdim).astype(BF16), 1, kp)
    wb = _pad_axis(_pad_axis(w.astype(BF16), 0, kp), 1, np_)
    tn = kw.pop("tn", 512)
    while np_ % tn:
        tn //= 2
    out = matmul_pallas(xb, wb, tn=tn, **kw)
    return out[:, :n].reshape(bsz, seq, n)


def _rwkv7_time_mix(p, shift_mu, w_up, w0, a_up, a0, g_up, k_k, k_a, r_k, gn_w, gn_b):
    bsz, seq, _ = p.shape
    p_prev = jnp.pad(p[:, :-1], ((0, 0), (1, 0), (0, 0)))
    p = p + (p_prev - p) * shift_mu
    r, k, v, xw, xa, xg = jnp.split(
        p, [B_WIDTH, 2 * B_WIDTH, 3 * B_WIDTH, 3 * B_WIDTH + B_DECAY_RANK,
            3 * B_WIDTH + B_DECAY_RANK + B_ICLR_RANK], axis=-1)
    w_log = -jax.nn.softplus(-(w0 + _mm3(jnp.tanh(xw), w_up))) - 0.5
    a = jax.nn.sigmoid(a0 + _mm3(xa, a_up))
    g = _mm3(jax.nn.sigmoid(xg), g_up)
    heads = lambda t: t.astype(F32).reshape(bsz, seq, B_HEADS, B_HEAD_DIM)
    r, k, v, a = heads(r), heads(k), heads(v), heads(a)
    log_decay = heads(-jnp.exp(w_log))
    kk = k * k_k.astype(F32).reshape(B_HEADS, B_HEAD_DIM)
    kk = kk * lax.rsqrt(jnp.maximum(jnp.sum(kk * kk, axis=-1, keepdims=True), 1e-24))
    k = k * (1.0 + (a - 1.0) * k_a.astype(F32).reshape(B_HEADS, B_HEAD_DIM))
    tr = lambda t: jnp.transpose(t, (0, 2, 1, 3))
    y = tr(wkv7_pallas(tr(r), tr(k), tr(v), tr(-kk), tr(kk * a), tr(log_decay)))
    mean = jnp.mean(y, axis=-1, keepdims=True)
    var = jnp.mean(jnp.square(y - mean), axis=-1, keepdims=True)
    y = ((y - mean) * lax.rsqrt(var + B_GN_EPS)).reshape(bsz, seq, B_WIDTH) * gn_w.astype(F32) + gn_b.astype(F32)
    bonus = jnp.sum(r * k * r_k.astype(F32), axis=-1, keepdims=True) * v
    y = y + bonus.reshape(bsz, seq, B_WIDTH)
    return y * g


CONV_TAIL = 8


def _ssd_kernel(x_ref, bm_ref, cm_ref, z_ref, wx_ref, wb_ref, wc_ref, bx_ref, bb_ref, bc_ref,
                dt_ref, dtt_ref, da_ref, dat_ref, dskip_ref, gn_ref, o_ref,
                state_ref, tx_ref, tb_ref, tc_ref, *, tb, hpg, hdim):
    step = pl.program_id(2)
    width = hpg * hdim

    @pl.when(step == 0)
    def _():
        state_ref[...] = jnp.zeros_like(state_ref)
        tx_ref[...] = jnp.zeros_like(tx_ref)
        tb_ref[...] = jnp.zeros_like(tb_ref)
        tc_ref[...] = jnp.zeros_like(tc_ref)

    def conv_silu(cur_ref, tail_ref, w_ref, b_ref):
        cur = cur_ref[...]
        ext = jnp.concatenate([tail_ref[...], cur], axis=0)
        w = w_ref[...]
        acc = b_ref[...]
        for j in range(C_CONV):
            off = CONV_TAIL - (C_CONV - 1) + j
            acc = acc + w[j:j + 1, :] * ext[off:off + tb, :]
        tail_ref[...] = cur[tb - CONV_TAIL:, :]
        return acc * jax.nn.sigmoid(acc)

    xs = conv_silu(x_ref, tx_ref, wx_ref, bx_ref)
    bm = conv_silu(bm_ref, tb_ref, wb_ref, bb_ref)
    cm = conv_silu(cm_ref, tc_ref, wc_ref, bc_ref)

    row = lax.broadcasted_iota(I32, (CHUNK, CHUNK), 0)
    col = lax.broadcasted_iota(I32, (CHUNK, CHUNK), 1)
    causal = row >= col
    tri = causal.astype(F32)
    triu = (row <= col).astype(F32)
    eh = (lax.broadcasted_iota(I32, (hpg, width), 1) // hdim == lax.broadcasted_iota(I32, (hpg, width), 0)).astype(F32)
    lane_in_pair = lax.broadcasted_iota(I32, (CHUNK, 2 * hdim), 1)

    ys = []
    for c in range(tb // CHUNK):
        lo = c * CHUNK
        xc = xs[lo:lo + CHUNK]
        bc = bm[lo:lo + CHUNK]
        cc = cm[lo:lo + CHUNK]
        dt = dt_ref[0, 0, lo:lo + CHUNK, :]
        dtt = dtt_ref[0, 0, :, lo:lo + CHUNK]
        cum = jnp.dot(tri, da_ref[0, 0, lo:lo + CHUNK, :], preferred_element_type=F32, precision=HI)
        cumt = jnp.dot(dat_ref[0, 0, :, lo:lo + CHUNK], triu, preferred_element_type=F32, precision=HI)
        cb = _bdot_nt(cc, bc)
        pairs = []
        for hp in range(hpg // 2):
            xp = xc[:, hp * 2 * hdim:(hp + 1) * 2 * hdim]
            outs = []
            for h in (2 * hp, 2 * hp + 1):
                seg = cum[:, h:h + 1] - cumt[h:h + 1, :]
                wts = cb * jnp.exp(jnp.where(causal, seg, -jnp.inf)) * dtt[h:h + 1, :]
                outs.append(_bdot(wts, xp))
            pairs.append(jnp.where(lane_in_pair < hdim, outs[0], outs[1]))
        y = jnp.concatenate(pairs, axis=1)
        ecum = jnp.exp(cum)
        y = y + _bdot(cc, state_ref[...]) * jnp.dot(ecum, eh, preferred_element_type=F32, precision=HI)
        last = cum[CHUNK - 1:CHUNK, :]
        to_end = jnp.exp(last - cum) * dt
        xw = xc * jnp.dot(to_end, eh, preferred_element_type=F32, precision=HI)
        sdec = jnp.dot(jnp.exp(last), eh, preferred_element_type=F32, precision=HI)
        state_ref[...] = state_ref[...] * sdec + _bdot_tn(bc, xw)
        ys.append(y)
    y = jnp.concatenate(ys, axis=0)
    y = y + dskip_ref[...] * xs
    z = z_ref[...].astype(F32)
    y = y * (z * jax.nn.sigmoid(z))
    y = y * lax.rsqrt(jnp.mean(y * y, axis=-1, keepdims=True) + NORM_EPS) * gn_ref[...]
    o_ref[...] = y.astype(o_ref.dtype)


def ssd_mixer_pallas(z, xbc, dt_raw, conv_w, conv_b, dt_bias, a_log, d_skip, gate_norm, bsz, seq, tb=256):
    m, c_inner = z.shape
    nh = dt_raw.shape[1]
    hpg = nh // C_GROUPS
    width = c_inner // C_GROUPS
    assert width == hpg * C_HEAD_DIM and width % LANES == 0 and C_STATE % LANES == 0 and hpg % 2 == 0
    steps = seq // tb
    dt = jax.nn.softplus(dt_raw.astype(F32) + dt_bias.astype(F32)).reshape(bsz, seq, C_GROUPS, hpg)
    da = dt * -jnp.exp(a_log.astype(F32)).reshape(C_GROUPS, hpg)
    dt_g = jnp.transpose(dt, (0, 2, 1, 3))
    da_g = jnp.transpose(da, (0, 2, 1, 3))
    dt_t = jnp.transpose(dt, (0, 2, 3, 1))
    da_t = jnp.transpose(da, (0, 2, 3, 1))
    dskip = jnp.repeat(d_skip.astype(F32), C_HEAD_DIM).reshape(C_GROUPS, 1, width)
    gn = gate_norm.astype(F32).reshape(C_GROUPS, 1, width)
    cw = conv_w.astype(F32)
    cbias = conv_b.astype(F32).reshape(1, -1)
    boff = c_inner // C_STATE
    coff = boff + C_GROUPS

    row_ix = lambda b, s: b * steps + s
    small = pl.BlockSpec((1, 1, tb, hpg), lambda b, g, s: (b, g, s, 0))
    small_t = pl.BlockSpec((1, 1, hpg, tb), lambda b, g, s: (b, g, 0, s))
    per_group = pl.BlockSpec((None, 1, width), lambda b, g, s: (g, 0, 0))
    in_specs = [
        pl.BlockSpec((tb, width), lambda b, g, s: (row_ix(b, s), g)),
        pl.BlockSpec((tb, C_STATE), lambda b, g, s: (row_ix(b, s), boff + g)),
        pl.BlockSpec((tb, C_STATE), lambda b, g, s: (row_ix(b, s), coff + g)),
        pl.BlockSpec((tb, width), lambda b, g, s: (row_ix(b, s), g)),
        pl.BlockSpec((C_CONV, width), lambda b, g, s: (0, g)),
        pl.BlockSpec((C_CONV, C_STATE), lambda b, g, s: (0, boff + g)),
        pl.BlockSpec((C_CONV, C_STATE), lambda b, g, s: (0, coff + g)),
        pl.BlockSpec((1, width), lambda b, g, s: (0, g)),
        pl.BlockSpec((1, C_STATE), lambda b, g, s: (0, boff + g)),
        pl.BlockSpec((1, C_STATE), lambda b, g, s: (0, coff + g)),
        small, small_t, small, small_t, per_group, per_group,
    ]
    return pl.pallas_call(
        functools.partial(_ssd_kernel, tb=tb, hpg=hpg, hdim=C_HEAD_DIM),
        grid=(bsz, C_GROUPS, steps),
        in_specs=in_specs,
        out_specs=pl.BlockSpec((tb, width), lambda b, g, s: (row_ix(b, s), g)),
        out_shape=jax.ShapeDtypeStruct((m, c_inner), BF16),
        scratch_shapes=[pltpu.VMEM((C_STATE, width), F32), pltpu.VMEM((CONV_TAIL, width), F32),
                        pltpu.VMEM((CONV_TAIL, C_STATE), F32), pltpu.VMEM((CONV_TAIL, C_STATE), F32)],
        compiler_params=pltpu.CompilerParams(
            dimension_semantics=("parallel", "parallel", "arbitrary"), vmem_limit_bytes=VMEM_LIMIT_BYTES),
        name="ssd",
    )(xbc, xbc, xbc, z, cw, cw, cw, cbias, cbias, cbias, dt_g, dt_t, da_g, da_t, dskip, gn)


def _dsa_rwkv_layer(x, positions, norm_mix, w_in, q_norm, w_uq, w_qi, kv_norm, w_uk, w_uv,
                    shift_mu, w_up, w0, a_up, a0, g_up, k_k, k_a, r_k, gn_w, gn_b, w_out,
                    norm_ffn, ffn_gate, ffn_up, ffn_down):
    bsz, seq, d = x.shape
    m = bsz * seq
    x2 = x.reshape(m, d)
    cos_a, sin_a = _rope_tables(positions, A_HEAD_DIM)
    cos_i, sin_i = _rope_tables(positions, IDX_DIM)

    mix_in = w_in.shape[1]
    w_in_b = _pad_axis(w_in.astype(BF16), 1, _round_up(mix_in, 512))
    proj = matmul_pallas(rmsnorm_pallas(x2, norm_mix), w_in_b, name="l0_in")[:, :mix_in]
    proj = proj.reshape(bsz, seq, mix_in)
    c_q, c_kv, k_idx, w_idx, p_b = jnp.split(
        proj, [A_Q_RANK, A_Q_RANK + A_KV_RANK, A_Q_RANK + A_KV_RANK + IDX_DIM, A_IN], axis=-1)
    c_q = _rmsnorm_jax(c_q, q_norm)
    c_kv = _rmsnorm_jax(c_kv, kv_norm)
    q = _apply_rope(_mm3(c_q, w_uq).reshape(bsz, seq, A_HEADS, A_HEAD_DIM), cos_a, sin_a)
    q_idx = _apply_rope(_mm3(c_q, w_qi).reshape(bsz, seq, IDX_HEADS, IDX_DIM), cos_i, sin_i)
    kv = _mm3(c_kv, jnp.concatenate([w_uk, w_uv], axis=1))
    k = _apply_rope(kv[:, :, None, :A_HEAD_DIM], cos_a, sin_a)[:, :, 0]
    v = kv[:, :, A_HEAD_DIM:]
    k_idx = _apply_rope(k_idx[:, :, None, :], cos_i, sin_i)[:, :, 0]
    w_idx = w_idx * (IDX_HEADS * IDX_DIM) ** -0.5
    y_a = dsa_pallas(q * A_HEAD_DIM ** -0.5, q_idx, w_idx, k, v, k_idx, topk=min(IDX_TOPK_MAX, seq // 4))
    y_b = _rwkv7_time_mix(p_b, shift_mu, w_up, w0, a_up, a0, g_up, k_k, k_a, r_k, gn_w, gn_b)
    y = jnp.concatenate([y_a, y_b], axis=-1).reshape(m, -1).astype(BF16)
    x2 = matmul_pallas(y, w_out.astype(BF16), res=x2, name="l0_out")

    ffn = ffn_gate.shape[1]
    ffn_p = _round_up(ffn, 512)
    h = rmsnorm_pallas(x2, norm_ffn)
    wg = _pad_axis(ffn_gate.astype(BF16), 1, ffn_p)[None]
    wu = _pad_axis(ffn_up.astype(BF16), 1, ffn_p)[None]
    wd = _pad_axis(ffn_down.astype(BF16), 0, ffn_p)
    tm = min(1024, m)
    hh = grouped_glu_pallas(h, wg, wu, jnp.zeros((m // tm,), I32), jnp.full((1,), m // tm, I32), tm=tm, name="l0_glu")
    x2 = matmul_pallas(hh, wd, res=x2, tm=1024, tn=1024, tk=ffn_p // 4, name="l0_down")
    return x2.reshape(bsz, seq, d)


def _ssd_moe_layer(x, norm_mix, w_in, conv_w, conv_b, dt_bias, a_log, d_skip, gate_norm, w_out,
                   norm_ffn, router, exp_gate, exp_up, exp_down):
    bsz, seq, d = x.shape
    m = bsz * seq
    x2 = x.reshape(m, d)
    c_inner = w_out.shape[0]
    c_conv_dim = c_inner + 2 * C_GROUPS * C_STATE
    h = rmsnorm_pallas(x2, norm_mix)
    w_in_b = w_in.astype(BF16)
    z = matmul_pallas(h, w_in_b[:, :c_inner], name="l1_in_z")
    xbc = matmul_pallas(h, w_in_b[:, c_inner:c_inner + c_conv_dim], name="l1_in_xbc")
    dt_raw = matmul_pallas(h, w_in_b[:, c_inner + c_conv_dim:], name="l1_in_dt")
    y = ssd_mixer_pallas(z, xbc, dt_raw, conv_w, conv_b, dt_bias, a_log, d_skip, gate_norm, bsz, seq)
    x2 = matmul_pallas(y, w_out.astype(BF16), res=x2, tm=1024, tn=1024, tk=2048, name="l1_out")
    x2 = moe_sparse(x2, norm_ffn, router, exp_gate, exp_up, exp_down)
    return x2.reshape(bsz, seq, d)


def kernel(x, positions, l0_norm_mix, l0_w_in, l0_q_norm, l0_w_uq, l0_w_qi, l0_kv_norm, l0_w_uk, l0_w_uv,
           l0_shift_mu, l0_w_up, l0_w0, l0_a_up, l0_a0, l0_g_up, l0_k_k, l0_k_a, l0_r_k, l0_gn_w, l0_gn_b,
           l0_w_out, l0_norm_ffn, l0_ffn_gate, l0_ffn_up, l0_ffn_down,
           l1_norm_mix, l1_w_in, l1_conv_w, l1_conv_b, l1_dt_bias, l1_a_log, l1_d_skip, l1_gate_norm, l1_w_out,
           l1_norm_ffn, l1_router, l1_exp_gate, l1_exp_up, l1_exp_down, final_norm):
    x = _dsa_rwkv_layer(x, positions, l0_norm_mix, l0_w_in, l0_q_norm, l0_w_uq, l0_w_qi, l0_kv_norm, l0_w_uk,
                        l0_w_uv, l0_shift_mu, l0_w_up, l0_w0, l0_a_up, l0_a0, l0_g_up, l0_k_k, l0_k_a, l0_r_k,
                        l0_gn_w, l0_gn_b, l0_w_out, l0_norm_ffn, l0_ffn_gate, l0_ffn_up, l0_ffn_down)
    x = _ssd_moe_layer(x, l1_norm_mix, l1_w_in, l1_conv_w, l1_conv_b, l1_dt_bias, l1_a_log, l1_d_skip,
                       l1_gate_norm, l1_w_out, l1_norm_ffn, l1_router, l1_exp_gate, l1_exp_up, l1_exp_down)
    bsz, seq, d = x.shape
    return rmsnorm_pallas(x.reshape(bsz * seq, d), final_norm, out_dtype=x.dtype).reshape(bsz, seq, d)
```

```python
import functools

import numpy as np
import jax
import jax.numpy as jnp
from jax import lax
from jax.experimental import pallas as pl
from jax.experimental.pallas import tpu as pltpu

F32 = jnp.float32
BF16 = jnp.bfloat16
I32 = jnp.int32
HI = lax.Precision.HIGHEST

CHUNK = 64
QBLOCK = 128
ROPE_THETA = 10000.0
NORM_EPS = 1e-6

A_HEADS = 16
A_HEAD_DIM = 128
A_Q_RANK = 1024
A_KV_RANK = 512
IDX_HEADS = 32
IDX_DIM = 128
IDX_TOPK_MAX = 256

B_HEADS = 32
B_HEAD_DIM = 64
B_WIDTH = B_HEADS * B_HEAD_DIM
B_DECAY_RANK = 128
B_ICLR_RANK = 128
B_GATE_RANK = 480
B_GN_EPS = 64e-5
A_IN = A_Q_RANK + A_KV_RANK + IDX_DIM + IDX_HEADS

C_HEAD_DIM = 64
C_GROUPS = 8
C_STATE = 128
C_CONV = 4

TOP_K_EXPERTS = 2

LANES = 128
VMEM_LIMIT_BYTES = 56 * 1024 * 1024


def _round_up(n, m):
    return (n + m - 1) // m * m


def _pad_axis(a, axis, size):
    if a.shape[axis] == size:
        return a
    pads = [(0, 0)] * a.ndim
    pads[axis] = (0, size - a.shape[axis])
    return jnp.pad(a, pads)


def _rmsnorm_kernel(x_ref, g_ref, o_ref):
    x = x_ref[...].astype(F32)
    ms = jnp.mean(x * x, axis=-1, keepdims=True)
    o_ref[...] = (x * lax.rsqrt(ms + NORM_EPS) * g_ref[...]).astype(o_ref.dtype)


def rmsnorm_pallas(x, g, out_dtype=BF16, tm=256):
    m, d = x.shape
    tm = min(tm, m)
    return pl.pallas_call(
        _rmsnorm_kernel,
        grid=(m // tm,),
        in_specs=[pl.BlockSpec((tm, d), lambda i: (i, 0)),
                  pl.BlockSpec((1, d), lambda i: (0, 0))],
        out_specs=pl.BlockSpec((tm, d), lambda i: (i, 0)),
        out_shape=jax.ShapeDtypeStruct((m, d), out_dtype),
        compiler_params=pltpu.CompilerParams(dimension_semantics=("parallel",)),
        name="rmsnorm",
    )(x, g.reshape(1, d).astype(F32))


def _mm_kernel(x_ref, w_ref, *rest, nk, has_res):
    if has_res:
        res_ref, o_ref, acc_ref = rest
    else:
        o_ref, acc_ref = rest
        res_ref = None
    p = jnp.dot(x_ref[...], w_ref[...], preferred_element_type=F32)

    def finish(acc):
        if has_res:
            acc = acc + res_ref[...].astype(F32)
        o_ref[...] = acc.astype(o_ref.dtype)

    if nk == 1:
        finish(p)
        return
    k = pl.program_id(2)

    @pl.when(k == 0)
    def _():
        acc_ref[...] = p

    @pl.when(jnp.logical_and(k > 0, k < nk - 1))
    def _():
        acc_ref[...] += p

    @pl.when(k == nk - 1)
    def _():
        finish(acc_ref[...] + p)


def matmul_pallas(x, w, res=None, out_dtype=F32, tm=1024, tn=512, tk=None, name="matmul"):
    m, kdim = x.shape
    k2, n = w.shape
    assert kdim == k2
    tm = min(tm, m)
    tn = min(tn, n)
    tk = kdim if tk is None else min(tk, kdim)
    assert m % tm == 0 and n % tn == 0 and kdim % tk == 0, (x.shape, w.shape, tm, tn, tk)
    nk = kdim // tk
    in_specs = [pl.BlockSpec((tm, tk), lambda i, j, k: (i, k)),
                pl.BlockSpec((tk, tn), lambda i, j, k: (k, j))]
    args = [x, w]
    if res is not None:
        in_specs.append(pl.BlockSpec((tm, tn), lambda i, j, k: (i, j)))
        args.append(res)
    acc_shape = (tm, tn) if nk > 1 else (8, 128)
    return pl.pallas_call(
        functools.partial(_mm_kernel, nk=nk, has_res=res is not None),
        grid=(m // tm, n // tn, nk),
        in_specs=in_specs,
        out_specs=pl.BlockSpec((tm, tn), lambda i, j, k: (i, j)),
        out_shape=jax.ShapeDtypeStruct((m, n), out_dtype),
        scratch_shapes=[pltpu.VMEM(acc_shape, F32)],
        compiler_params=pltpu.CompilerParams(
            dimension_semantics=("parallel", "parallel", "arbitrary"),
            vmem_limit_bytes=VMEM_LIMIT_BYTES),
        name=name,
    )(*args)


def _mm_rope_kernel(x_ref, w_ref, cos_ref, sin_ref, o_ref, *, scale, head_dim):
    p = jnp.dot(x_ref[...], w_ref[...], preferred_element_type=F32) * scale
    cos = cos_ref[...]
    sin = sin_ref[...]
    heads = []
    for h in range(p.shape[1] // head_dim):
        ph = p[:, h * head_dim:(h + 1) * head_dim]
        heads.append(ph * cos + pltpu.roll(ph, head_dim // 2, 1) * sin)
    o_ref[...] = jnp.concatenate(heads, axis=1).astype(o_ref.dtype)


def matmul_rope_pallas(x, w, cos_full, sin_signed, scale=1.0, tm=1024, tn=512, name="matmul_rope"):
    m, kdim = x.shape
    n = w.shape[1]
    hd = cos_full.shape[1]
    tm = min(tm, m)
    assert m % tm == 0 and n % tn == 0 and tn % hd == 0
    return pl.pallas_call(
        functools.partial(_mm_rope_kernel, scale=scale, head_dim=hd),
        grid=(m // tm, n // tn),
        in_specs=[pl.BlockSpec((tm, kdim), lambda i, j: (i, 0)),
                  pl.BlockSpec((kdim, tn), lambda i, j: (0, j)),
                  pl.BlockSpec((tm, hd), lambda i, j: (i, 0)),
                  pl.BlockSpec((tm, hd), lambda i, j: (i, 0))],
        out_specs=pl.BlockSpec((tm, tn), lambda i, j: (i, j)),
        out_shape=jax.ShapeDtypeStruct((m, n), BF16),
        compiler_params=pltpu.CompilerParams(
            dimension_semantics=("parallel", "parallel"), vmem_limit_bytes=VMEM_LIMIT_BYTES),
        name=name,
    )(x, w, cos_full, sin_signed)


SHIFT_TAIL = 8


def _mm_shift_kernel(x_ref, w_ref, mu_ref, o_ref, carry_ref, *, tiles_per_seq):
    i = pl.program_id(0)
    j = pl.program_id(1)

    @pl.when(i == 0)
    def _():
        carry_ref[j] = jnp.zeros(carry_ref.shape[1:], F32)

    p = jnp.dot(x_ref[...], w_ref[...], preferred_element_type=F32)
    tm = p.shape[0]
    tail = jnp.where(i % tiles_per_seq == 0, 0.0, carry_ref[j])
    prev = jnp.concatenate([tail, p], axis=0)[SHIFT_TAIL - 1:SHIFT_TAIL - 1 + tm]
    carry_ref[j] = p[tm - SHIFT_TAIL:]
    o_ref[...] = p + (prev - p) * mu_ref[...]


def matmul_shift_pallas(x, w, mu, seq, tm=1024, tn=256, name="matmul_shift"):
    m, kdim = x.shape
    n = w.shape[1]
    tm = min(tm, seq)
    assert seq % tm == 0 and m % seq == 0 and n % tn == 0
    return pl.pallas_call(
        functools.partial(_mm_shift_kernel, tiles_per_seq=seq // tm),
        grid=(m // tm, n // tn),
        in_specs=[pl.BlockSpec((tm, kdim), lambda i, j: (i, 0)),
                  pl.BlockSpec((kdim, tn), lambda i, j: (0, j)),
                  pl.BlockSpec((1, tn), lambda i, j: (0, j))],
        out_specs=pl.BlockSpec((tm, tn), lambda i, j: (i, j)),
        out_shape=jax.ShapeDtypeStruct((m, n), F32),
        scratch_shapes=[pltpu.VMEM((n // tn, SHIFT_TAIL, tn), F32)],
        compiler_params=pltpu.CompilerParams(
            dimension_semantics=("arbitrary", "arbitrary"), vmem_limit_bytes=VMEM_LIMIT_BYTES),
        name=name,
    )(x, w, mu.reshape(1, n).astype(F32))


def _gglu_kernel(te_ref, nu_ref, x_ref, wg_ref, wu_ref, o_ref):
    t = pl.program_id(0)

    @pl.when(t < nu_ref[0])
    def _():
        x = x_ref[...]
        g = jnp.dot(x, wg_ref[...], preferred_element_type=F32)
        u = jnp.dot(x, wu_ref[...], preferred_element_type=F32)
        o_ref[...] = (g * jax.nn.sigmoid(g) * u).astype(o_ref.dtype)

    @pl.when(t >= nu_ref[0])
    def _():
        o_ref[...] = jnp.zeros_like(o_ref)


def grouped_glu_pallas(xs, wg, wu, tile_expert, n_used, tm, tn=512, name="glu"):
    r, kdim = xs.shape
    n = wg.shape[2]
    assert r % tm == 0 and n % tn == 0
    grid_spec = pltpu.PrefetchScalarGridSpec(
        num_scalar_prefetch=2,
        grid=(r // tm, n // tn),
        in_specs=[pl.BlockSpec((tm, kdim), lambda t, j, te, nu: (t, 0)),
                  pl.BlockSpec((None, kdim, tn), lambda t, j, te, nu: (te[t], 0, j)),
                  pl.BlockSpec((None, kdim, tn), lambda t, j, te, nu: (te[t], 0, j))],
        out_specs=pl.BlockSpec((tm, tn), lambda t, j, te, nu: (t, j)),
    )
    return pl.pallas_call(
        _gglu_kernel,
        grid_spec=grid_spec,
        out_shape=jax.ShapeDtypeStruct((r, n), BF16),
        compiler_params=pltpu.CompilerParams(
            dimension_semantics=("arbitrary", "arbitrary"), vmem_limit_bytes=VMEM_LIMIT_BYTES),
        name=name,
    )(tile_expert, n_used, xs, wg, wu)


def _gdown_kernel(te_ref, nu_ref, x_ref, w_ref, o_ref):
    t = pl.program_id(0)

    @pl.when(t < nu_ref[0])
    def _():
        o_ref[...] = jnp.dot(x_ref[...], w_ref[...], preferred_element_type=F32).astype(o_ref.dtype)

    @pl.when(t >= nu_ref[0])
    def _():
        o_ref[...] = jnp.zeros_like(o_ref)


def grouped_down_pallas(hs, wd, tile_expert, n_used, tm, tn=512, name="down"):
    r, kdim = hs.shape
    n = wd.shape[2]
    assert r % tm == 0 and n % tn == 0
    grid_spec = pltpu.PrefetchScalarGridSpec(
        num_scalar_prefetch=2,
        grid=(r // tm, n // tn),
        in_specs=[pl.BlockSpec((tm, kdim), lambda t, j, te, nu: (t, 0)),
                  pl.BlockSpec((None, kdim, tn), lambda t, j, te, nu: (te[t], 0, j))],
        out_specs=pl.BlockSpec((tm, tn), lambda t, j, te, nu: (t, j)),
    )
    return pl.pallas_call(
        _gdown_kernel,
        grid_spec=grid_spec,
        out_shape=jax.ShapeDtypeStruct((r, n), F32),
        compiler_params=pltpu.CompilerParams(
            dimension_semantics=("arbitrary", "arbitrary"), vmem_limit_bytes=VMEM_LIMIT_BYTES),
        name=name,
    )(tile_expert, n_used, hs, wd)


def _router_kernel(x_ref, g_ref, wr_ref, h_ref, r_ref, *, n_experts):
    x = x_ref[...]
    h = x * lax.rsqrt(jnp.mean(x * x, axis=-1, keepdims=True) + NORM_EPS) * g_ref[...]
    h_ref[...] = h.astype(h_ref.dtype)
    logits = jnp.dot(h, wr_ref[...], preferred_element_type=F32, precision=HI)
    lane = lax.broadcasted_iota(I32, logits.shape, 1)
    logits = jnp.where(lane < n_experts, logits, -jnp.inf)
    m1 = jnp.max(logits, axis=1, keepdims=True)
    i1 = jnp.min(jnp.where(logits == m1, lane, LANES), axis=1, keepdims=True)
    rest = jnp.where(lane == i1, -jnp.inf, logits)
    m2 = jnp.max(rest, axis=1, keepdims=True)
    i2 = jnp.min(jnp.where(rest == m2, lane, LANES), axis=1, keepdims=True)
    e2 = jnp.exp(m2 - m1)
    w1 = 1.0 / (1.0 + e2)
    w2 = e2 / (1.0 + e2)
    out = jnp.where(lane == 0, i1.astype(F32), 0.0)
    out = jnp.where(lane == 1, i2.astype(F32), out)
    out = jnp.where(lane == 2, w1, out)
    out = jnp.where(lane == 3, w2, out)
    r_ref[...] = out


def router_pallas(x, g, router, tm=256):
    m, d = x.shape
    ne = router.shape[1]
    wr = jnp.pad(router.astype(F32), ((0, 0), (0, LANES - ne)))
    return pl.pallas_call(
        functools.partial(_router_kernel, n_experts=ne),
        grid=(m // tm,),
        in_specs=[pl.BlockSpec((tm, d), lambda i: (i, 0)),
                  pl.BlockSpec((1, d), lambda i: (0, 0)),
                  pl.BlockSpec((d, LANES), lambda i: (0, 0))],
        out_specs=[pl.BlockSpec((tm, d), lambda i: (i, 0)),
                   pl.BlockSpec((tm, LANES), lambda i: (i, 0))],
        out_shape=[jax.ShapeDtypeStruct((m, d), BF16), jax.ShapeDtypeStruct((m, LANES), F32)],
        compiler_params=pltpu.CompilerParams(dimension_semantics=("parallel",)),
        name="router",
    )(x, g.reshape(1, d).astype(F32), wr)


def moe_sparse(x2, norm_ffn, router, exp_gate, exp_up, exp_down, tm=512):
    m, d = x2.shape
    ne = router.shape[1]
    h, route = router_pallas(x2, norm_ffn, router)
    idx = route[:, :TOP_K_EXPERTS].astype(I32)
    wts = route[:, TOP_K_EXPERTS:2 * TOP_K_EXPERTS]
    flat_e = idx.reshape(-1)
    onehot = (flat_e[:, None] == jnp.arange(ne, dtype=I32)[None, :]).astype(I32)
    rank = jnp.sum((jnp.cumsum(onehot, axis=0) - 1) * onehot, axis=1)
    count = jnp.sum(onehot, axis=0)
    padded = (count + tm - 1) // tm * tm
    ends = jnp.cumsum(padded)
    pos = (ends - padded)[flat_e] + rank
    rows = m * TOP_K_EXPERTS + ne * tm
    n_tiles = rows // tm
    token_of_row = jnp.zeros((rows,), I32).at[pos].set(jnp.arange(m * TOP_K_EXPERTS, dtype=I32) // TOP_K_EXPERTS)
    tile_start = jnp.arange(n_tiles, dtype=I32)[:, None] * tm
    tile_expert = jnp.minimum(jnp.sum((tile_start >= ends[None, :]).astype(I32), axis=1), ne - 1).astype(I32)
    n_used = (ends[-1] // tm).astype(I32).reshape(1)
    xs = jnp.take(h, token_of_row, axis=0)
    hh = grouped_glu_pallas(xs, exp_gate.astype(BF16), exp_up.astype(BF16), tile_expert, n_used, tm, name="moe_glu")
    ys = grouped_down_pallas(hh, exp_down.astype(BF16), tile_expert, n_used, tm, name="moe_down")
    pos2 = pos.reshape(m, TOP_K_EXPERTS)
    y = jnp.take(ys, pos2[:, 0], axis=0) * wts[:, 0:1] + jnp.take(ys, pos2[:, 1], axis=0) * wts[:, 1:2]
    return x2 + y


def _rmsnorm_jax(x, g):
    xf = x.astype(F32)
    return xf * lax.rsqrt(jnp.mean(xf * xf, axis=-1, keepdims=True) + NORM_EPS) * g.astype(F32)


def _rope_tables(positions, dim):
    inv_freq = ROPE_THETA ** (-jnp.arange(0, dim, 2, dtype=F32) / dim)
    ang = positions.astype(F32)[..., None] * inv_freq
    return jnp.cos(ang)[:, :, None, :], jnp.sin(ang)[:, :, None, :]


def _apply_rope(x, cos, sin):
    x1, x2 = jnp.split(x.astype(F32), 2, axis=-1)
    return jnp.concatenate([x1 * cos - x2 * sin, x2 * cos + x1 * sin], axis=-1)


NEG_BIG = -1e30
INT_MIN = np.int32(-2 ** 31)


def _dsa_kernel(qi_ref, w_ref, q_ref, kit_ref, kt_ref, v_ref, o_ref,
                s_ref, wb_ref, m_ref, l_ref, acc_ref, qis_ref, qs_ref, *, tk, idx_heads, att_heads, topk, head_group):
    i = pl.program_id(1)
    hd_i = qi_ref.shape[2] // idx_heads
    for h in range(idx_heads):
        qis_ref[h * QBLOCK:(h + 1) * QBLOCK, :] = qi_ref[0, :, h * hd_i:(h + 1) * hd_i]
    hd_a = q_ref.shape[2] // att_heads
    for h in range(att_heads):
        qs_ref[h * QBLOCK:(h + 1) * QBLOCK, :] = q_ref[0, :, h * hd_a:(h + 1) * hd_a]
    n_tiles = ((i + 1) * QBLOCK + tk - 1) // tk
    nl = tk // LANES
    row = lax.broadcasted_iota(I32, (QBLOCK, tk), 0)
    col = lax.broadcasted_iota(I32, (QBLOCK, tk), 1)
    limit = (2 * i + 1 + (row >= CHUNK).astype(I32)) * CHUNK

    def admissible(t):
        return (t * tk + col) < limit

    def fold_lanes(x, op):
        out = x[:, :LANES]
        for j in range(1, nl):
            out = op(out, x[:, j * LANES:(j + 1) * LANES])
        return out

    w = w_ref[0, 0]
    for h in range(idx_heads):
        wb_ref[h] = jnp.broadcast_to(w[:, h:h + 1], (QBLOCK, LANES))

    def idx_body(t, carry):
        kit = kit_ref[0, t]
        acc = jnp.zeros((QBLOCK, tk), F32)
        for g in range(idx_heads // head_group):
            rows = head_group * QBLOCK
            rel = jnp.dot(qis_ref[g * rows:(g + 1) * rows, :], kit, preferred_element_type=F32)
            for hh in range(head_group):
                h = g * head_group + hh
                relu = jnp.maximum(rel[hh * QBLOCK:(hh + 1) * QBLOCK], 0.0)
                acc = acc + jnp.concatenate([wb_ref[h]] * nl, axis=1) * relu
        score = jnp.where(admissible(t), acc, -jnp.inf)
        bits = lax.bitcast_convert_type(score, I32)
        s_ref[t] = bits ^ ((bits >> 31) & np.int32(0x7FFFFFFF))
        return carry

    lax.fori_loop(0, n_tiles, idx_body, 0)

    def bit_body(it, ans):
        cand = ans | lax.shift_left(np.int32(1), 31 - it)
        cand_s = cand ^ INT_MIN

        def cnt_body(t, cnt):
            return cnt + fold_lanes((s_ref[t] >= cand_s).astype(I32), jnp.add)

        cnt = lax.fori_loop(0, n_tiles, cnt_body, jnp.zeros((QBLOCK, LANES), I32))
        total = jnp.sum(cnt, axis=1, keepdims=True)
        return jnp.where(total >= topk, cand, ans)

    ans = lax.fori_loop(0, 32, bit_body, jnp.zeros((QBLOCK, 1), I32))
    thr = ans ^ INT_MIN

    def masked_logits(t):
        sel = jnp.logical_and(s_ref[t] >= thr, admissible(t))
        bias = jnp.where(sel, 0.0, NEG_BIG)
        logits = jnp.dot(qs_ref[...], kt_ref[0, t], preferred_element_type=F32)
        return logits + jnp.concatenate([bias] * att_heads, axis=0)

    m_ref[...] = jnp.full_like(m_ref, NEG_BIG)

    def max_body(t, carry):
        m_ref[...] = jnp.maximum(m_ref[...], fold_lanes(masked_logits(t), jnp.maximum))
        return carry

    lax.fori_loop(0, n_tiles, max_body, 0)
    m_ref[...] = jnp.broadcast_to(jnp.max(m_ref[...], axis=1, keepdims=True), m_ref.shape)

    l_ref[...] = jnp.zeros_like(l_ref)
    acc_ref[...] = jnp.zeros_like(acc_ref)

    def att_body(t, carry):
        p = jnp.exp(masked_logits(t) - jnp.concatenate([m_ref[...]] * nl, axis=1))
        l_ref[...] += fold_lanes(p, jnp.add)
        acc_ref[...] += jnp.dot(p.astype(BF16), v_ref[0, t], preferred_element_type=F32)
        return carry

    lax.fori_loop(0, n_tiles, att_body, 0)
    out = acc_ref[...] / jnp.sum(l_ref[...], axis=1, keepdims=True)
    d = out.shape[1]
    for h in range(att_heads):
        o_ref[0, :, h * d:(h + 1) * d] = out[h * QBLOCK:(h + 1) * QBLOCK, :]


def dsa_pallas(q, q_idx, w_idx, k, v, k_idx, topk, ha, hi, tk=512, head_group=8):
    bsz, seq, _ = q.shape
    d = q.shape[2] // ha
    di = q_idx.shape[2] // hi
    nb = seq // QBLOCK
    nt = seq // tk
    wb = w_idx.astype(F32).reshape(bsz, nb, QBLOCK, hi)
    kit = jnp.transpose(k_idx.astype(BF16).reshape(bsz, nt, tk, di), (0, 1, 3, 2))
    kt = jnp.transpose(k.astype(BF16).reshape(bsz, nt, tk, d), (0, 1, 3, 2))
    vb = v.astype(BF16).reshape(bsz, nt, tk, d)
    return pl.pallas_call(
        functools.partial(_dsa_kernel, tk=tk, idx_heads=hi, att_heads=ha, topk=topk, head_group=head_group),
        grid=(bsz, nb),
        in_specs=[
            pl.BlockSpec((1, QBLOCK, hi * di), lambda b, i: (b, i, 0)),
            pl.BlockSpec((1, 1, QBLOCK, hi), lambda b, i: (b, i, 0, 0)),
            pl.BlockSpec((1, QBLOCK, ha * d), lambda b, i: (b, i, 0)),
            pl.BlockSpec((1, nt, di, tk), lambda b, i: (b, 0, 0, 0)),
            pl.BlockSpec((1, nt, d, tk), lambda b, i: (b, 0, 0, 0)),
            pl.BlockSpec((1, nt, tk, d), lambda b, i: (b, 0, 0, 0)),
        ],
        out_specs=pl.BlockSpec((1, QBLOCK, ha * d), lambda b, i: (b, i, 0)),
        out_shape=jax.ShapeDtypeStruct((bsz, seq, ha * d), F32),
        scratch_shapes=[
            pltpu.VMEM((nt, QBLOCK, tk), I32),
            pltpu.VMEM((hi, QBLOCK, LANES), F32),
            pltpu.VMEM((ha * QBLOCK, LANES), F32),
            pltpu.VMEM((ha * QBLOCK, LANES), F32),
            pltpu.VMEM((ha * QBLOCK, d), F32),
            pltpu.VMEM((hi * QBLOCK, di), BF16),
            pltpu.VMEM((ha * QBLOCK, d), BF16),
        ],
        compiler_params=pltpu.CompilerParams(
            dimension_semantics=("parallel", "arbitrary"),
            vmem_limit_bytes=VMEM_LIMIT_BYTES),
        name="dsa",
    )(q_idx, wb, q, kit, kt, vb)


def _bdot(a, b):
    return jnp.dot(a.astype(BF16), b.astype(BF16), preferred_element_type=F32)


def _bdot_nt(a, b):
    return lax.dot_general(a.astype(BF16), b.astype(BF16), (((1,), (1,)), ((), ())), preferred_element_type=F32)


def _bdot_tn(a, b):
    return lax.dot_general(a.astype(BF16), b.astype(BF16), (((0,), (0,)), ((), ())), preferred_element_type=F32)


def _wkv7_kernel(r_ref, k_ref, v_ref, a_ref, lw_ref, g_ref, kk_ref, ka_ref, rk_ref, gnw_ref, gnb_ref,
                 y_ref, z_ref, p_ref, *, heads, clen):
    c = pl.program_id(2)
    row = lax.broadcasted_iota(I32, (clen, clen), 0)
    col = lax.broadcasted_iota(I32, (clen, clen), 1)
    strict = row > col
    incl = row >= col
    tri = incl.astype(F32)
    eye = (row == col).astype(F32)

    @pl.when(c == 0)
    def _():
        z_ref[...] = jnp.zeros_like(z_ref)
        p_ref[...] = jnp.ones_like(p_ref)

    hs = range(heads)
    cols = [slice(h * clen, (h + 1) * clen) for h in hs]
    r = [r_ref[0, :, cols[h]] for h in hs]
    k0 = [k_ref[0, :, cols[h]] for h in hs]
    v = [v_ref[0, :, cols[h]] for h in hs]
    ai = [a_ref[0, :, cols[h]] for h in hs]
    lw = [lw_ref[0, :, cols[h]] for h in hs]
    kk = [k0[h] * kk_ref[:, cols[h]] for h in hs]
    kk = [kk[h] * lax.rsqrt(jnp.maximum(jnp.sum(kk[h] * kk[h], axis=-1, keepdims=True), 1e-24)) for h in hs]
    k = [k0[h] * (1.0 + (ai[h] - 1.0) * ka_ref[:, cols[h]]) for h in hs]
    cum = [jnp.dot(tri, lw[h], preferred_element_type=F32, precision=HI) for h in hs]
    ecum = [jnp.exp(cum[h]) for h in hs]
    encum = [jnp.exp(-cum[h]) for h in hs]
    rt = [r[h] * ecum[h] for h in hs]
    kt = [k[h] * encum[h] for h in hs]
    bt = [kk[h] * ai[h] * encum[h] for h in hs]
    at = [-kk[h] * jnp.exp(cum[h] - lw[h]) for h in hs]
    x = [jnp.concatenate([at[h], rt[h]], axis=0) for h in hs]
    aab = [_bdot_nt(x[h], bt[h]) for h in hs]
    aak = [_bdot_nt(x[h], kt[h]) for h in hs]
    a_ab = [jnp.where(strict, aab[h][:clen], 0.0) for h in hs]
    a_ak = [jnp.where(strict, aak[h][:clen], 0.0) for h in hs]
    a_rb = [jnp.where(incl, aab[h][clen:], 0.0) for h in hs]
    a_rk = [jnp.where(incl, aak[h][clen:], 0.0) for h in hs]
    xp = a_ab
    t = [eye + a_ab[h] for h in hs]
    n = 2
    while n < clen:
        xp = [_bdot(xp[h], xp[h]) for h in hs]
        t = [t[h] + _bdot(t[h], xp[h]) for h in hs]
        n *= 2
    av = [_bdot(a_ak[h], v[h]) for h in hs]
    uv = [_bdot(t[h], av[h]) for h in hs]
    at2 = [_bdot(t[h], at[h]) for h in hs]
    rp = [rt[h] + _bdot(a_rb[h], at2[h]) for h in hs]
    yv = [_bdot(a_rb[h], uv[h]) + _bdot(a_rk[h], v[h]) for h in hs]
    mp = [_bdot_tn(bt[h], at2[h]) for h in hs]
    gp = [_bdot_tn(jnp.concatenate([bt[h], kt[h]], axis=0), jnp.concatenate([uv[h], v[h]], axis=0)) for h in hs]
    lhs = [jnp.concatenate([rp[h], eye + mp[h]], axis=0) * p_ref[h] for h in hs]
    out = [_bdot(lhs[h], z_ref[h]) for h in hs]
    for h in hs:
        z_ref[h] = out[h][clen:] + gp[h]
        p_ref[h] = ecum[h][clen - 1:clen, :]
    y = [out[h][:clen] + yv[h] for h in hs]
    mean = [jnp.mean(y[h], axis=-1, keepdims=True) for h in hs]
    var = [jnp.mean(jnp.square(y[h] - mean[h]), axis=-1, keepdims=True) for h in hs]
    yn = [(y[h] - mean[h]) * lax.rsqrt(var[h] + B_GN_EPS) * gnw_ref[:, cols[h]] + gnb_ref[:, cols[h]] for h in hs]
    bonus = [jnp.sum(r[h] * k[h] * rk_ref[:, cols[h]], axis=-1, keepdims=True) * v[h] for h in hs]
    y_ref[0] = (jnp.concatenate([yn[h] + bonus[h] for h in hs], axis=1) * g_ref[0]).astype(y_ref.dtype)


def wkv7_pallas(proj, rkv_col, a, lw, g, k_k, k_a, r_k, gn_w, gn_b, heads_per_step=32, clen=CHUNK):
    bsz, seq, width = a.shape
    nh = width // clen
    gh = min(heads_per_step, nh)
    bw = gh * clen
    assert width % bw == 0 and rkv_col % bw == 0
    nblk = width // bw

    def col_spec(first_block):
        return pl.BlockSpec((1, clen, bw), lambda bi, hi, ci: (bi, ci, first_block + hi))

    par_spec = pl.BlockSpec((1, bw), lambda bi, hi, ci: (0, hi))
    par = lambda t: t.astype(F32).reshape(1, width)
    r0 = rkv_col // bw
    return pl.pallas_call(
        functools.partial(_wkv7_kernel, heads=gh, clen=clen),
        grid=(bsz, nblk, seq // clen),
        in_specs=[col_spec(r0), col_spec(r0 + nblk), col_spec(r0 + 2 * nblk),
                  col_spec(0), col_spec(0), col_spec(0)] + [par_spec] * 5,
        out_specs=col_spec(0),
        out_shape=jax.ShapeDtypeStruct((bsz, seq, width), BF16),
        scratch_shapes=[pltpu.VMEM((gh, clen, clen), F32), pltpu.VMEM((gh, 1, clen), F32)],
        compiler_params=pltpu.CompilerParams(dimension_semantics=("parallel", "parallel", "arbitrary")),
        name="wkv7",
    )(proj, proj, proj, a, lw, g, par(k_k), par(k_a), par(r_k), par(gn_w), par(gn_b))


def _mm3(x, w, **kw):
    bsz, seq, kdim = x.shape
    kp = _round_up(kdim, 128)
    n = w.shape[1]
    np_ = _round_up(n, 128)
    xb = _pad_axis(x.reshape(bsz * seq, kdim).astype(BF16), 1, kp)
    wb = _pad_axis(_pad_axis(w.astype(BF16), 0, kp), 1, np_)
    tn = kw.pop("tn", 512)
    while np_ % tn:
        tn //= 2
    out = matmul_pallas(xb, wb, tn=tn, **kw)
    return out[:, :n].reshape(bsz, seq, n)


def _rwkv7_time_mix(proj, b_col, w_up, w0, a_up, a0, g_up, k_k, k_a, r_k, gn_w, gn_b):
    lo = b_col + 3 * B_WIDTH
    xw = proj[:, :, lo:lo + B_DECAY_RANK]
    xa = proj[:, :, lo + B_DECAY_RANK:lo + B_DECAY_RANK + B_ICLR_RANK]
    xg = proj[:, :, lo + B_DECAY_RANK + B_ICLR_RANK:lo + B_DECAY_RANK + B_ICLR_RANK + B_GATE_RANK]
    w_log = -jax.nn.softplus(-(w0 + _mm3(jnp.tanh(xw), w_up))) - 0.5
    a = jax.nn.sigmoid(a0 + _mm3(xa, a_up))
    g = _mm3(jax.nn.sigmoid(xg), g_up)
    return wkv7_pallas(proj, b_col, a, -jnp.exp(w_log), g, k_k, k_a, r_k, gn_w, gn_b)


CONV_TAIL = 8


def _ssd_kernel(x_ref, bm_ref, cm_ref, z_ref, wx_ref, wb_ref, wc_ref, bx_ref, bb_ref, bc_ref,
                dt_ref, dtt_ref, da_ref, dat_ref, dskip_ref, gn_ref, o_ref,
                state_ref, tx_ref, tb_ref, tc_ref, *, tb, hpg, hdim):
    step = pl.program_id(2)
    width = hpg * hdim

    @pl.when(step == 0)
    def _():
        state_ref[...] = jnp.zeros_like(state_ref)
        tx_ref[...] = jnp.zeros_like(tx_ref)
        tb_ref[...] = jnp.zeros_like(tb_ref)
        tc_ref[...] = jnp.zeros_like(tc_ref)

    def conv_silu(cur_ref, tail_ref, w_ref, b_ref):
        cur = cur_ref[...]
        ext = jnp.concatenate([tail_ref[...], cur], axis=0)
        w = w_ref[...]
        acc = b_ref[...]
        for j in range(C_CONV):
            off = CONV_TAIL - (C_CONV - 1) + j
            acc = acc + w[j:j + 1, :] * ext[off:off + tb, :]
        tail_ref[...] = cur[tb - CONV_TAIL:, :]
        return acc * jax.nn.sigmoid(acc)

    xs = conv_silu(x_ref, tx_ref, wx_ref, bx_ref)
    bm = conv_silu(bm_ref, tb_ref, wb_ref, bb_ref)
    cm = conv_silu(cm_ref, tc_ref, wc_ref, bc_ref)

    row = lax.broadcasted_iota(I32, (CHUNK, CHUNK), 0)
    col = lax.broadcasted_iota(I32, (CHUNK, CHUNK), 1)
    causal = row >= col
    tri = causal.astype(F32)
    triu = (row <= col).astype(F32)
    eh = (lax.broadcasted_iota(I32, (hpg, width), 1) // hdim == lax.broadcasted_iota(I32, (hpg, width), 0)).astype(F32)
    lane_in_pair = lax.broadcasted_iota(I32, (CHUNK, 2 * hdim), 1)

    ys = []
    for c in range(tb // CHUNK):
        lo = c * CHUNK
        xc = xs[lo:lo + CHUNK]
        bc = bm[lo:lo + CHUNK]
        cc = cm[lo:lo + CHUNK]
        dt = dt_ref[0, 0, lo:lo + CHUNK, :]
        dtt = dtt_ref[0, 0, :, lo:lo + CHUNK]
        cum = jnp.dot(tri, da_ref[0, 0, lo:lo + CHUNK, :], preferred_element_type=F32, precision=HI)
        cumt = jnp.dot(dat_ref[0, 0, :, lo:lo + CHUNK], triu, preferred_element_type=F32, precision=HI)
        cb = _bdot_nt(cc, bc)
        pairs = []
        for hp in range(hpg // 2):
            xp = xc[:, hp * 2 * hdim:(hp + 1) * 2 * hdim]
            outs = []
            for h in (2 * hp, 2 * hp + 1):
                seg = cum[:, h:h + 1] - cumt[h:h + 1, :]
                wts = cb * jnp.exp(jnp.where(causal, seg, -jnp.inf)) * dtt[h:h + 1, :]
                outs.append(_bdot(wts, xp))
            pairs.append(jnp.where(lane_in_pair < hdim, outs[0], outs[1]))
        y = jnp.concatenate(pairs, axis=1)
        ecum = jnp.exp(cum)
        y = y + _bdot(cc, state_ref[...]) * jnp.dot(ecum, eh, preferred_element_type=F32, precision=HI)
        last = cum[CHUNK - 1:CHUNK, :]
        to_end = jnp.exp(last - cum) * dt
        xw = xc * jnp.dot(to_end, eh, preferred_element_type=F32, precision=HI)
        sdec = jnp.dot(jnp.exp(last), eh, preferred_element_type=F32, precision=HI)
        state_ref[...] = state_ref[...] * sdec + _bdot_tn(bc, xw)
        ys.append(y)
    y = jnp.concatenate(ys, axis=0)
    y = y + dskip_ref[...] * xs
    z = z_ref[...].astype(F32)
    y = y * (z * jax.nn.sigmoid(z))
    y = y * lax.rsqrt(jnp.mean(y * y, axis=-1, keepdims=True) + NORM_EPS) * gn_ref[...]
    o_ref[...] = y.astype(o_ref.dtype)


def ssd_mixer_pallas(z, xbc, dt_raw, conv_w, conv_b, dt_bias, a_log, d_skip, gate_norm, bsz, seq, tb=256):
    m, c_inner = z.shape
    nh = dt_raw.shape[1]
    hpg = nh // C_GROUPS
    width = c_inner // C_GROUPS
    assert width == hpg * C_HEAD_DIM and width % LANES == 0 and C_STATE % LANES == 0 and hpg % 2 == 0
    steps = seq // tb
    dt = jax.nn.softplus(dt_raw.astype(F32) + dt_bias.astype(F32)).reshape(bsz, seq, C_GROUPS, hpg)
    da = dt * -jnp.exp(a_log.astype(F32)).reshape(C_GROUPS, hpg)
    dt_g = jnp.transpose(dt, (0, 2, 1, 3))
    da_g = jnp.transpose(da, (0, 2, 1, 3))
    dt_t = jnp.transpose(dt, (0, 2, 3, 1))
    da_t = jnp.transpose(da, (0, 2, 3, 1))
    dskip = jnp.repeat(d_skip.astype(F32), C_HEAD_DIM).reshape(C_GROUPS, 1, width)
    gn = gate_norm.astype(F32).reshape(C_GROUPS, 1, width)
    cw = conv_w.astype(F32)
    cbias = conv_b.astype(F32).reshape(1, -1)
    boff = c_inner // C_STATE
    coff = boff + C_GROUPS

    row_ix = lambda b, s: b * steps + s
    small = pl.BlockSpec((1, 1, tb, hpg), lambda b, g, s: (b, g, s, 0))
    small_t = pl.BlockSpec((1, 1, hpg, tb), lambda b, g, s: (b, g, 0, s))
    per_group = pl.BlockSpec((None, 1, width), lambda b, g, s: (g, 0, 0))
    in_specs = [
        pl.BlockSpec((tb, width), lambda b, g, s: (row_ix(b, s), g)),
        pl.BlockSpec((tb, C_STATE), lambda b, g, s: (row_ix(b, s), boff + g)),
        pl.BlockSpec((tb, C_STATE), lambda b, g, s: (row_ix(b, s), coff + g)),
        pl.BlockSpec((tb, width), lambda b, g, s: (row_ix(b, s), g)),
        pl.BlockSpec((C_CONV, width), lambda b, g, s: (0, g)),
        pl.BlockSpec((C_CONV, C_STATE), lambda b, g, s: (0, boff + g)),
        pl.BlockSpec((C_CONV, C_STATE), lambda b, g, s: (0, coff + g)),
        pl.BlockSpec((1, width), lambda b, g, s: (0, g)),
        pl.BlockSpec((1, C_STATE), lambda b, g, s: (0, boff + g)),
        pl.BlockSpec((1, C_STATE), lambda b, g, s: (0, coff + g)),
        small, small_t, small, small_t, per_group, per_group,
    ]
    return pl.pallas_call(
        functools.partial(_ssd_kernel, tb=tb, hpg=hpg, hdim=C_HEAD_DIM),
        grid=(bsz, C_GROUPS, steps),
        in_specs=in_specs,
        out_specs=pl.BlockSpec((tb, width), lambda b, g, s: (row_ix(b, s), g)),
        out_shape=jax.ShapeDtypeStruct((m, c_inner), BF16),
        scratch_shapes=[pltpu.VMEM((C_STATE, width), F32), pltpu.VMEM((CONV_TAIL, width), F32),
                        pltpu.VMEM((CONV_TAIL, C_STATE), F32), pltpu.VMEM((CONV_TAIL, C_STATE), F32)],
        compiler_params=pltpu.CompilerParams(
            dimension_semantics=("parallel", "parallel", "arbitrary"), vmem_limit_bytes=VMEM_LIMIT_BYTES),
        name="ssd",
    )(xbc, xbc, xbc, z, cw, cw, cw, cbias, cbias, cbias, dt_g, dt_t, da_g, da_t, dskip, gn)


def _dsa_rwkv_layer(x, positions, norm_mix, w_in, q_norm, w_uq, w_qi, kv_norm, w_uk, w_uv,
                    shift_mu, w_up, w0, a_up, a0, g_up, k_k, k_a, r_k, gn_w, gn_b, w_out,
                    norm_ffn, ffn_gate, ffn_up, ffn_down):
    bsz, seq, d = x.shape
    m = bsz * seq
    x2 = x.reshape(m, d)
    cos_a, sin_a = _rope_tables(positions, A_HEAD_DIM)
    cos_i, sin_i = _rope_tables(positions, IDX_DIM)

    b_in = w_in.shape[1] - A_IN
    b_col = _round_up(A_IN, 8 * B_HEAD_DIM)
    width = _round_up(b_col + b_in, 256)
    w_in_b = w_in.astype(BF16)
    w_in_p = jnp.concatenate([_pad_axis(w_in_b[:, :A_IN], 1, b_col), _pad_axis(w_in_b[:, A_IN:], 1, width - b_col)], axis=1)
    mu_p = jnp.concatenate([jnp.zeros((b_col,), F32), _pad_axis(shift_mu.astype(F32), 0, width - b_col)])
    proj = matmul_shift_pallas(rmsnorm_pallas(x2, norm_mix), w_in_p, mu_p, seq, name="l0_in").reshape(bsz, seq, width)
    c_q = proj[:, :, :A_Q_RANK]
    c_kv = proj[:, :, A_Q_RANK:A_Q_RANK + A_KV_RANK]
    k_idx = proj[:, :, A_Q_RANK + A_KV_RANK:A_Q_RANK + A_KV_RANK + IDX_DIM]
    w_idx = proj[:, :, A_Q_RANK + A_KV_RANK + IDX_DIM:A_IN]
    c_q = _rmsnorm_jax(c_q, q_norm)
    c_kv = _rmsnorm_jax(c_kv, kv_norm)
    c_q2 = c_q.reshape(m, A_Q_RANK).astype(BF16)
    rope_cs = lambda cos, sin: (jnp.concatenate([cos, cos], -1).reshape(m, -1), jnp.concatenate([-sin, sin], -1).reshape(m, -1))
    q = matmul_rope_pallas(c_q2, w_uq.astype(BF16), *rope_cs(cos_a, sin_a), scale=A_HEAD_DIM ** -0.5, name="l0_q")
    q_idx = matmul_rope_pallas(c_q2, w_qi.astype(BF16), *rope_cs(cos_i, sin_i), name="l0_qidx")
    kv = _mm3(c_kv, jnp.concatenate([w_uk, w_uv], axis=1))
    k = _apply_rope(kv[:, :, None, :A_HEAD_DIM], cos_a, sin_a)[:, :, 0]
    v = kv[:, :, A_HEAD_DIM:]
    k_idx = _apply_rope(k_idx[:, :, None, :], cos_i, sin_i)[:, :, 0]
    w_idx = w_idx * (IDX_HEADS * IDX_DIM) ** -0.5
    y_a = dsa_pallas(q.reshape(bsz, seq, -1), q_idx.reshape(bsz, seq, -1), w_idx, k, v, k_idx,
                     topk=min(IDX_TOPK_MAX, seq // 4), ha=A_HEADS, hi=IDX_HEADS)
    y_b = _rwkv7_time_mix(proj, b_col, w_up, w0, a_up, a0, g_up, k_k, k_a, r_k, gn_w, gn_b)
    y = jnp.concatenate([y_a.astype(BF16), y_b], axis=-1).reshape(m, -1)
    x2 = matmul_pallas(y, w_out.astype(BF16), res=x2, name="l0_out")

    ffn = ffn_gate.shape[1]
    ffn_p = _round_up(ffn, 512)
    h = rmsnorm_pallas(x2, norm_ffn)
    wg = _pad_axis(ffn_gate.astype(BF16), 1, ffn_p)[None]
    wu = _pad_axis(ffn_up.astype(BF16), 1, ffn_p)[None]
    wd = _pad_axis(ffn_down.astype(BF16), 0, ffn_p)
    tm = min(1024, m)
    hh = grouped_glu_pallas(h, wg, wu, jnp.zeros((m // tm,), I32), jnp.full((1,), m // tm, I32), tm=tm, name="l0_glu")
    x2 = matmul_pallas(hh, wd, res=x2, tm=1024, tn=1024, tk=ffn_p // 4, name="l0_down")
    return x2.reshape(bsz, seq, d)


def _ssd_moe_layer(x, norm_mix, w_in, conv_w, conv_b, dt_bias, a_log, d_skip, gate_norm, w_out,
                   norm_ffn, router, exp_gate, exp_up, exp_down):
    bsz, seq, d = x.shape
    m = bsz * seq
    x2 = x.reshape(m, d)
    c_inner = w_out.shape[0]
    c_conv_dim = c_inner + 2 * C_GROUPS * C_STATE
    h = rmsnorm_pallas(x2, norm_mix)
    w_in_b = w_in.astype(BF16)
    z = matmul_pallas(h, w_in_b[:, :c_inner], name="l1_in_z")
    xbc = matmul_pallas(h, w_in_b[:, c_inner:c_inner + c_conv_dim], name="l1_in_xbc")
    dt_raw = matmul_pallas(h, w_in_b[:, c_inner + c_conv_dim:], name="l1_in_dt")
    y = ssd_mixer_pallas(z, xbc, dt_raw, conv_w, conv_b, dt_bias, a_log, d_skip, gate_norm, bsz, seq)
    x2 = matmul_pallas(y, w_out.astype(BF16), res=x2, tm=1024, tn=1024, tk=2048, name="l1_out")
    x2 = moe_sparse(x2, norm_ffn, router, exp_gate, exp_up, exp_down)
    return x2.reshape(bsz, seq, d)


def kernel(x, positions, l0_norm_mix, l0_w_in, l0_q_norm, l0_w_uq, l0_w_qi, l0_kv_norm, l0_w_uk, l0_w_uv,
           l0_shift_mu, l0_w_up, l0_w0, l0_a_up, l0_a0, l0_g_up, l0_k_k, l0_k_a, l0_r_k, l0_gn_w, l0_gn_b,
           l0_w_out, l0_norm_ffn, l0_ffn_gate, l0_ffn_up, l0_ffn_down,
           l1_norm_mix, l1_w_in, l1_conv_w, l1_conv_b, l1_dt_bias, l1_a_log, l1_d_skip, l1_gate_norm, l1_w_out,
           l1_norm_ffn, l1_router, l1_exp_gate, l1_exp_up, l1_exp_down, final_norm):
    x = _dsa_rwkv_layer(x, positions, l0_norm_mix, l0_w_in, l0_q_norm, l0_w_uq, l0_w_qi, l0_kv_norm, l0_w_uk,
                        l0_w_uv, l0_shift_mu, l0_w_up, l0_w0, l0_a_up, l0_a0, l0_g_up, l0_k_k, l0_k_a, l0_r_k,
                        l0_gn_w, l0_gn_b, l0_w_out, l0_norm_ffn, l0_ffn_gate, l0_ffn_up, l0_ffn_down)
    x = _ssd_moe_layer(x, l1_norm_mix, l1_w_in, l1_conv_w, l1_conv_b, l1_dt_bias, l1_a_log, l1_d_skip,
                       l1_gate_norm, l1_w_out, l1_norm_ffn, l1_router, l1_exp_gate, l1_exp_up, l1_exp_down)
    bsz, seq, d = x.shape
    return rmsnorm_pallas(x.reshape(bsz * seq, d), final_norm, out_dtype=x.dtype).reshape(bsz, seq, d)
```

```python
import functools

import numpy as np
import jax
import jax.numpy as jnp
from jax import lax
from jax.experimental import pallas as pl
from jax.experimental.pallas import tpu as pltpu

F32 = jnp.float32
BF16 = jnp.bfloat16
I32 = jnp.int32
HI = lax.Precision.HIGHEST

CHUNK = 64
QBLOCK = 128
ROPE_THETA = 10000.0
NORM_EPS = 1e-6

A_HEADS = 16
A_HEAD_DIM = 128
A_Q_RANK = 1024
A_KV_RANK = 512
IDX_HEADS = 32
IDX_DIM = 128
IDX_TOPK_MAX = 256

B_HEADS = 32
B_HEAD_DIM = 64
B_WIDTH = B_HEADS * B_HEAD_DIM
B_DECAY_RANK = 128
B_ICLR_RANK = 128
B_GATE_RANK = 480
B_GN_EPS = 64e-5
A_IN = A_Q_RANK + A_KV_RANK + IDX_DIM + IDX_HEADS

C_HEAD_DIM = 64
C_GROUPS = 8
C_STATE = 128
C_CONV = 4

TOP_K_EXPERTS = 2

LANES = 128
VMEM_LIMIT_BYTES = 56 * 1024 * 1024


def _round_up(n, m):
    return (n + m - 1) // m * m


def _pad_axis(a, axis, size):
    if a.shape[axis] == size:
        return a
    pads = [(0, 0)] * a.ndim
    pads[axis] = (0, size - a.shape[axis])
    return jnp.pad(a, pads)


def _rmsnorm_kernel(x_ref, g_ref, o_ref):
    x = x_ref[...].astype(F32)
    ms = jnp.mean(x * x, axis=-1, keepdims=True)
    o_ref[...] = (x * lax.rsqrt(ms + NORM_EPS) * g_ref[...]).astype(o_ref.dtype)


def rmsnorm_pallas(x, g, out_dtype=BF16, tm=256):
    m, d = x.shape
    tm = min(tm, m)
    return pl.pallas_call(
        _rmsnorm_kernel,
        grid=(m // tm,),
        in_specs=[pl.BlockSpec((tm, d), lambda i: (i, 0)),
                  pl.BlockSpec((1, d), lambda i: (0, 0))],
        out_specs=pl.BlockSpec((tm, d), lambda i: (i, 0)),
        out_shape=jax.ShapeDtypeStruct((m, d), out_dtype),
        compiler_params=pltpu.CompilerParams(dimension_semantics=("parallel",)),
        name="rmsnorm",
    )(x, g.reshape(1, d).astype(F32))


def _mm_kernel(x_ref, w_ref, *rest, nk, has_res):
    if has_res:
        res_ref, o_ref, acc_ref = rest
    else:
        o_ref, acc_ref = rest
        res_ref = None
    p = jnp.dot(x_ref[...], w_ref[...], preferred_element_type=F32)

    def finish(acc):
        if has_res:
            acc = acc + res_ref[...].astype(F32)
        o_ref[...] = acc.astype(o_ref.dtype)

    if nk == 1:
        finish(p)
        return
    k = pl.program_id(2)

    @pl.when(k == 0)
    def _():
        acc_ref[...] = p

    @pl.when(jnp.logical_and(k > 0, k < nk - 1))
    def _():
        acc_ref[...] += p

    @pl.when(k == nk - 1)
    def _():
        finish(acc_ref[...] + p)


def matmul_pallas(x, w, res=None, out_dtype=F32, tm=1024, tn=512, tk=None, name="matmul"):
    m, kdim = x.shape
    k2, n = w.shape
    assert kdim == k2
    tm = min(tm, m)
    tn = min(tn, n)
    tk = kdim if tk is None else min(tk, kdim)
    assert m % tm == 0 and n % tn == 0 and kdim % tk == 0, (x.shape, w.shape, tm, tn, tk)
    nk = kdim // tk
    in_specs = [pl.BlockSpec((tm, tk), lambda i, j, k: (i, k)),
                pl.BlockSpec((tk, tn), lambda i, j, k: (k, j))]
    args = [x, w]
    if res is not None:
        in_specs.append(pl.BlockSpec((tm, tn), lambda i, j, k: (i, j)))
        args.append(res)
    acc_shape = (tm, tn) if nk > 1 else (8, 128)
    return pl.pallas_call(
        functools.partial(_mm_kernel, nk=nk, has_res=res is not None),
        grid=(m // tm, n // tn, nk),
        in_specs=in_specs,
        out_specs=pl.BlockSpec((tm, tn), lambda i, j, k: (i, j)),
        out_shape=jax.ShapeDtypeStruct((m, n), out_dtype),
        scratch_shapes=[pltpu.VMEM(acc_shape, F32)],
        compiler_params=pltpu.CompilerParams(
            dimension_semantics=("parallel", "parallel", "arbitrary"),
            vmem_limit_bytes=VMEM_LIMIT_BYTES),
        name=name,
    )(*args)


def _mm_rope_kernel(x_ref, w_ref, cos_ref, sin_ref, o_ref, *, scale, head_dim):
    p = jnp.dot(x_ref[...], w_ref[...], preferred_element_type=F32) * scale
    cos = cos_ref[...]
    sin = sin_ref[...]
    heads = []
    for h in range(p.shape[1] // head_dim):
        ph = p[:, h * head_dim:(h + 1) * head_dim]
        heads.append(ph * cos + pltpu.roll(ph, head_dim // 2, 1) * sin)
    o_ref[...] = jnp.concatenate(heads, axis=1).astype(o_ref.dtype)


def matmul_rope_pallas(x, w, cos_full, sin_signed, scale=1.0, tm=1024, tn=512, name="matmul_rope"):
    m, kdim = x.shape
    n = w.shape[1]
    hd = cos_full.shape[1]
    tm = min(tm, m)
    assert m % tm == 0 and n % tn == 0 and tn % hd == 0
    return pl.pallas_call(
        functools.partial(_mm_rope_kernel, scale=scale, head_dim=hd),
        grid=(m // tm, n // tn),
        in_specs=[pl.BlockSpec((tm, kdim), lambda i, j: (i, 0)),
                  pl.BlockSpec((kdim, tn), lambda i, j: (0, j)),
                  pl.BlockSpec((tm, hd), lambda i, j: (i, 0)),
                  pl.BlockSpec((tm, hd), lambda i, j: (i, 0))],
        out_specs=pl.BlockSpec((tm, tn), lambda i, j: (i, j)),
        out_shape=jax.ShapeDtypeStruct((m, n), BF16),
        compiler_params=pltpu.CompilerParams(
            dimension_semantics=("parallel", "parallel"), vmem_limit_bytes=VMEM_LIMIT_BYTES),
        name=name,
    )(x, w, cos_full, sin_signed)


SHIFT_TAIL = 8


def _mm_shift_kernel(x_ref, w_ref, mu_ref, o_ref, carry_ref, *, tiles_per_seq):
    i = pl.program_id(0)
    j = pl.program_id(1)

    @pl.when(i == 0)
    def _():
        carry_ref[j] = jnp.zeros(carry_ref.shape[1:], F32)

    p = jnp.dot(x_ref[...], w_ref[...], preferred_element_type=F32)
    tm = p.shape[0]
    tail = jnp.where(i % tiles_per_seq == 0, 0.0, carry_ref[j])
    prev = jnp.concatenate([tail, p], axis=0)[SHIFT_TAIL - 1:SHIFT_TAIL - 1 + tm]
    carry_ref[j] = p[tm - SHIFT_TAIL:]
    o_ref[...] = p + (prev - p) * mu_ref[...]


def matmul_shift_pallas(x, w, mu, seq, tm=1024, tn=256, name="matmul_shift"):
    m, kdim = x.shape
    n = w.shape[1]
    tm = min(tm, seq)
    assert seq % tm == 0 and m % seq == 0 and n % tn == 0
    return pl.pallas_call(
        functools.partial(_mm_shift_kernel, tiles_per_seq=seq // tm),
        grid=(m // tm, n // tn),
        in_specs=[pl.BlockSpec((tm, kdim), lambda i, j: (i, 0)),
                  pl.BlockSpec((kdim, tn), lambda i, j: (0, j)),
                  pl.BlockSpec((1, tn), lambda i, j: (0, j))],
        out_specs=pl.BlockSpec((tm, tn), lambda i, j: (i, j)),
        out_shape=jax.ShapeDtypeStruct((m, n), F32),
        scratch_shapes=[pltpu.VMEM((n // tn, SHIFT_TAIL, tn), F32)],
        compiler_params=pltpu.CompilerParams(
            dimension_semantics=("arbitrary", "arbitrary"), vmem_limit_bytes=VMEM_LIMIT_BYTES),
        name=name,
    )(x, w, mu.reshape(1, n).astype(F32))


def _gglu_kernel(te_ref, nu_ref, x_ref, wg_ref, wu_ref, o_ref):
    t = pl.program_id(0)

    @pl.when(t < nu_ref[0])
    def _():
        x = x_ref[...]
        g = jnp.dot(x, wg_ref[...], preferred_element_type=F32)
        u = jnp.dot(x, wu_ref[...], preferred_element_type=F32)
        o_ref[...] = (g * jax.nn.sigmoid(g) * u).astype(o_ref.dtype)

    @pl.when(t >= nu_ref[0])
    def _():
        o_ref[...] = jnp.zeros_like(o_ref)


def grouped_glu_pallas(xs, wg, wu, tile_expert, n_used, tm, tn=512, name="glu"):
    r, kdim = xs.shape
    n = wg.shape[2]
    assert r % tm == 0 and n % tn == 0
    grid_spec = pltpu.PrefetchScalarGridSpec(
        num_scalar_prefetch=2,
        grid=(r // tm, n // tn),
        in_specs=[pl.BlockSpec((tm, kdim), lambda t, j, te, nu: (t, 0)),
                  pl.BlockSpec((None, kdim, tn), lambda t, j, te, nu: (te[t], 0, j)),
                  pl.BlockSpec((None, kdim, tn), lambda t, j, te, nu: (te[t], 0, j))],
        out_specs=pl.BlockSpec((tm, tn), lambda t, j, te, nu: (t, j)),
    )
    return pl.pallas_call(
        _gglu_kernel,
        grid_spec=grid_spec,
        out_shape=jax.ShapeDtypeStruct((r, n), BF16),
        compiler_params=pltpu.CompilerParams(
            dimension_semantics=("arbitrary", "arbitrary"), vmem_limit_bytes=VMEM_LIMIT_BYTES),
        name=name,
    )(tile_expert, n_used, xs, wg, wu)


def _gdown_kernel(te_ref, nu_ref, x_ref, w_ref, o_ref):
    t = pl.program_id(0)

    @pl.when(t < nu_ref[0])
    def _():
        o_ref[...] = jnp.dot(x_ref[...], w_ref[...], preferred_element_type=F32).astype(o_ref.dtype)

    @pl.when(t >= nu_ref[0])
    def _():
        o_ref[...] = jnp.zeros_like(o_ref)


def grouped_down_pallas(hs, wd, tile_expert, n_used, tm, tn=512, name="down"):
    r, kdim = hs.shape
    n = wd.shape[2]
    assert r % tm == 0 and n % tn == 0
    grid_spec = pltpu.PrefetchScalarGridSpec(
        num_scalar_prefetch=2,
        grid=(r // tm, n // tn),
        in_specs=[pl.BlockSpec((tm, kdim), lambda t, j, te, nu: (t, 0)),
                  pl.BlockSpec((None, kdim, tn), lambda t, j, te, nu: (te[t], 0, j))],
        out_specs=pl.BlockSpec((tm, tn), lambda t, j, te, nu: (t, j)),
    )
    return pl.pallas_call(
        _gdown_kernel,
        grid_spec=grid_spec,
        out_shape=jax.ShapeDtypeStruct((r, n), BF16),
        compiler_params=pltpu.CompilerParams(
            dimension_semantics=("arbitrary", "arbitrary"), vmem_limit_bytes=VMEM_LIMIT_BYTES),
        name=name,
    )(tile_expert, n_used, hs, wd)


def _router_kernel(x_ref, g_ref, wr_ref, h_ref, r_ref, *, n_experts):
    x = x_ref[...]
    h = x * lax.rsqrt(jnp.mean(x * x, axis=-1, keepdims=True) + NORM_EPS) * g_ref[...]
    h_ref[...] = h.astype(h_ref.dtype)
    logits = jnp.dot(h, wr_ref[...], preferred_element_type=F32, precision=HI)
    lane = lax.broadcasted_iota(I32, logits.shape, 1)
    logits = jnp.where(lane < n_experts, logits, -jnp.inf)
    m1 = jnp.max(logits, axis=1, keepdims=True)
    i1 = jnp.min(jnp.where(logits == m1, lane, LANES), axis=1, keepdims=True)
    rest = jnp.where(lane == i1, -jnp.inf, logits)
    m2 = jnp.max(rest, axis=1, keepdims=True)
    i2 = jnp.min(jnp.where(rest == m2, lane, LANES), axis=1, keepdims=True)
    e2 = jnp.exp(m2 - m1)
    w1 = 1.0 / (1.0 + e2)
    w2 = e2 / (1.0 + e2)
    out = jnp.where(lane == 0, i1.astype(F32), 0.0)
    out = jnp.where(lane == 1, i2.astype(F32), out)
    out = jnp.where(lane == 2, w1, out)
    out = jnp.where(lane == 3, w2, out)
    r_ref[...] = out


def router_pallas(x, g, router, tm=256):
    m, d = x.shape
    ne = router.shape[1]
    wr = jnp.pad(router.astype(F32), ((0, 0), (0, LANES - ne)))
    return pl.pallas_call(
        functools.partial(_router_kernel, n_experts=ne),
        grid=(m // tm,),
        in_specs=[pl.BlockSpec((tm, d), lambda i: (i, 0)),
                  pl.BlockSpec((1, d), lambda i: (0, 0)),
                  pl.BlockSpec((d, LANES), lambda i: (0, 0))],
        out_specs=[pl.BlockSpec((tm, d), lambda i: (i, 0)),
                   pl.BlockSpec((tm, LANES), lambda i: (i, 0))],
        out_shape=[jax.ShapeDtypeStruct((m, d), BF16), jax.ShapeDtypeStruct((m, LANES), F32)],
        compiler_params=pltpu.CompilerParams(dimension_semantics=("parallel",)),
        name="router",
    )(x, g.reshape(1, d).astype(F32), wr)


def moe_sparse(x2, norm_ffn, router, exp_gate, exp_up, exp_down, tm=512):
    m, d = x2.shape
    ne = router.shape[1]
    h, route = router_pallas(x2, norm_ffn, router)
    idx = route[:, :TOP_K_EXPERTS].astype(I32)
    wts = route[:, TOP_K_EXPERTS:2 * TOP_K_EXPERTS]
    flat_e = idx.reshape(-1)
    onehot = (flat_e[:, None] == jnp.arange(ne, dtype=I32)[None, :]).astype(I32)
    rank = jnp.sum((jnp.cumsum(onehot, axis=0) - 1) * onehot, axis=1)
    count = jnp.sum(onehot, axis=0)
    padded = (count + tm - 1) // tm * tm
    ends = jnp.cumsum(padded)
    pos = (ends - padded)[flat_e] + rank
    rows = m * TOP_K_EXPERTS + ne * tm
    n_tiles = rows // tm
    token_of_row = jnp.zeros((rows,), I32).at[pos].set(jnp.arange(m * TOP_K_EXPERTS, dtype=I32) // TOP_K_EXPERTS)
    tile_start = jnp.arange(n_tiles, dtype=I32)[:, None] * tm
    tile_expert = jnp.minimum(jnp.sum((tile_start >= ends[None, :]).astype(I32), axis=1), ne - 1).astype(I32)
    n_used = (ends[-1] // tm).astype(I32).reshape(1)
    xs = jnp.take(h, token_of_row, axis=0)
    hh = grouped_glu_pallas(xs, exp_gate.astype(BF16), exp_up.astype(BF16), tile_expert, n_used, tm, name="moe_glu")
    ys = grouped_down_pallas(hh, exp_down.astype(BF16), tile_expert, n_used, tm, name="moe_down")
    pos2 = pos.reshape(m, TOP_K_EXPERTS)
    y = (jnp.take(ys, pos2[:, 0], axis=0).astype(F32) * wts[:, 0:1]
         + jnp.take(ys, pos2[:, 1], axis=0).astype(F32) * wts[:, 1:2])
    return x2 + y


def _rmsnorm_jax(x, g):
    xf = x.astype(F32)
    return xf * lax.rsqrt(jnp.mean(xf * xf, axis=-1, keepdims=True) + NORM_EPS) * g.astype(F32)


def _rope_tables(positions, dim):
    inv_freq = ROPE_THETA ** (-jnp.arange(0, dim, 2, dtype=F32) / dim)
    ang = positions.astype(F32)[..., None] * inv_freq
    return jnp.cos(ang)[:, :, None, :], jnp.sin(ang)[:, :, None, :]


def _apply_rope(x, cos, sin):
    x1, x2 = jnp.split(x.astype(F32), 2, axis=-1)
    return jnp.concatenate([x1 * cos - x2 * sin, x2 * cos + x1 * sin], axis=-1)


NEG_BIG = -1e30
INT_MIN = np.int32(-2 ** 31)


def _dsa_kernel(qi_ref, w_ref, q_ref, kit_ref, kt_ref, v_ref, o_ref,
                s_ref, wb_ref, m_ref, l_ref, acc_ref, qis_ref, qs_ref, *, tk, idx_heads, att_heads, topk, head_group):
    i = pl.program_id(1)
    hd_i = qi_ref.shape[2] // idx_heads
    for h in range(idx_heads):
        qis_ref[h * QBLOCK:(h + 1) * QBLOCK, :] = qi_ref[0, :, h * hd_i:(h + 1) * hd_i]
    hd_a = q_ref.shape[2] // att_heads
    for h in range(att_heads):
        qs_ref[h * QBLOCK:(h + 1) * QBLOCK, :] = q_ref[0, :, h * hd_a:(h + 1) * hd_a]
    n_tiles = ((i + 1) * QBLOCK + tk - 1) // tk
    nl = tk // LANES
    row = lax.broadcasted_iota(I32, (QBLOCK, tk), 0)
    col = lax.broadcasted_iota(I32, (QBLOCK, tk), 1)
    limit = (2 * i + 1 + (row >= CHUNK).astype(I32)) * CHUNK

    def admissible(t):
        return (t * tk + col) < limit

    def fold_lanes(x, op):
        out = x[:, :LANES]
        for j in range(1, nl):
            out = op(out, x[:, j * LANES:(j + 1) * LANES])
        return out

    w = w_ref[0, 0]
    for h in range(idx_heads):
        wb_ref[h] = jnp.broadcast_to(w[:, h:h + 1], (QBLOCK, LANES))

    def idx_body(t, carry):
        kit = kit_ref[0, t]
        acc = jnp.zeros((QBLOCK, tk), F32)
        for g in range(idx_heads // head_group):
            rows = head_group * QBLOCK
            rel = jnp.dot(qis_ref[g * rows:(g + 1) * rows, :], kit, preferred_element_type=F32)
            for hh in range(head_group):
                h = g * head_group + hh
                relu = jnp.maximum(rel[hh * QBLOCK:(hh + 1) * QBLOCK], 0.0)
                acc = acc + jnp.concatenate([wb_ref[h]] * nl, axis=1) * relu
        score = jnp.where(admissible(t), acc, -jnp.inf)
        bits = lax.bitcast_convert_type(score, I32)
        s_ref[t] = bits ^ ((bits >> 31) & np.int32(0x7FFFFFFF))
        return carry

    lax.fori_loop(0, n_tiles, idx_body, 0)

    def bit_body(it, ans):
        cand = ans | lax.shift_left(np.int32(1), 31 - it)
        cand_s = cand ^ INT_MIN

        def cnt_body(t, cnt):
            return cnt + fold_lanes((s_ref[t] >= cand_s).astype(I32), jnp.add)

        cnt = lax.fori_loop(0, n_tiles, cnt_body, jnp.zeros((QBLOCK, LANES), I32))
        total = jnp.sum(cnt, axis=1, keepdims=True)
        return jnp.where(total >= topk, cand, ans)

    ans = lax.fori_loop(0, 32, bit_body, jnp.zeros((QBLOCK, 1), I32))
    thr = ans ^ INT_MIN

    def masked_logits(t):
        sel = jnp.logical_and(s_ref[t] >= thr, admissible(t))
        bias = jnp.where(sel, 0.0, NEG_BIG)
        logits = jnp.dot(qs_ref[...], kt_ref[0, t], preferred_element_type=F32)
        return logits + jnp.concatenate([bias] * att_heads, axis=0)

    m_ref[...] = jnp.full_like(m_ref, NEG_BIG)

    def max_body(t, carry):
        m_ref[...] = jnp.maximum(m_ref[...], fold_lanes(masked_logits(t), jnp.maximum))
        return carry

    lax.fori_loop(0, n_tiles, max_body, 0)
    m_ref[...] = jnp.broadcast_to(jnp.max(m_ref[...], axis=1, keepdims=True), m_ref.shape)

    l_ref[...] = jnp.zeros_like(l_ref)
    acc_ref[...] = jnp.zeros_like(acc_ref)

    def att_body(t, carry):
        p = jnp.exp(masked_logits(t) - jnp.concatenate([m_ref[...]] * nl, axis=1))
        l_ref[...] += fold_lanes(p, jnp.add)
        acc_ref[...] += jnp.dot(p.astype(BF16), v_ref[0, t], preferred_element_type=F32)
        return carry

    lax.fori_loop(0, n_tiles, att_body, 0)
    out = acc_ref[...] / jnp.sum(l_ref[...], axis=1, keepdims=True)
    d = out.shape[1]
    for h in range(att_heads):
        o_ref[0, :, h * d:(h + 1) * d] = out[h * QBLOCK:(h + 1) * QBLOCK, :]


def dsa_pallas(q, q_idx, w_idx, k, v, k_idx, topk, ha, hi, tk=512, head_group=8):
    bsz, seq, _ = q.shape
    d = q.shape[2] // ha
    di = q_idx.shape[2] // hi
    nb = seq // QBLOCK
    nt = seq // tk
    wb = w_idx.astype(F32).reshape(bsz, nb, QBLOCK, hi)
    kit = jnp.transpose(k_idx.astype(BF16).reshape(bsz, nt, tk, di), (0, 1, 3, 2))
    kt = jnp.transpose(k.astype(BF16).reshape(bsz, nt, tk, d), (0, 1, 3, 2))
    vb = v.astype(BF16).reshape(bsz, nt, tk, d)
    return pl.pallas_call(
        functools.partial(_dsa_kernel, tk=tk, idx_heads=hi, att_heads=ha, topk=topk, head_group=head_group),
        grid=(bsz, nb),
        in_specs=[
            pl.BlockSpec((1, QBLOCK, hi * di), lambda b, i: (b, i, 0)),
            pl.BlockSpec((1, 1, QBLOCK, hi), lambda b, i: (b, i, 0, 0)),
            pl.BlockSpec((1, QBLOCK, ha * d), lambda b, i: (b, i, 0)),
            pl.BlockSpec((1, nt, di, tk), lambda b, i: (b, 0, 0, 0)),
            pl.BlockSpec((1, nt, d, tk), lambda b, i: (b, 0, 0, 0)),
            pl.BlockSpec((1, nt, tk, d), lambda b, i: (b, 0, 0, 0)),
        ],
        out_specs=pl.BlockSpec((1, QBLOCK, ha * d), lambda b, i: (b, i, 0)),
        out_shape=jax.ShapeDtypeStruct((bsz, seq, ha * d), F32),
        scratch_shapes=[
            pltpu.VMEM((nt, QBLOCK, tk), I32),
            pltpu.VMEM((hi, QBLOCK, LANES), F32),
            pltpu.VMEM((ha * QBLOCK, LANES), F32),
            pltpu.VMEM((ha * QBLOCK, LANES), F32),
            pltpu.VMEM((ha * QBLOCK, d), F32),
            pltpu.VMEM((hi * QBLOCK, di), BF16),
            pltpu.VMEM((ha * QBLOCK, d), BF16),
        ],
        compiler_params=pltpu.CompilerParams(
            dimension_semantics=("parallel", "arbitrary"),
            vmem_limit_bytes=VMEM_LIMIT_BYTES),
        name="dsa",
    )(q_idx, wb, q, kit, kt, vb)


def _bdot(a, b):
    return jnp.dot(a.astype(BF16), b.astype(BF16), preferred_element_type=F32)


def _bdot_nt(a, b):
    return lax.dot_general(a.astype(BF16), b.astype(BF16), (((1,), (1,)), ((), ())), preferred_element_type=F32)


def _bdot_tn(a, b):
    return lax.dot_general(a.astype(BF16), b.astype(BF16), (((0,), (0,)), ((), ())), preferred_element_type=F32)


def _split3(x):
    hi = x.astype(BF16)
    r1 = x - hi.astype(F32)
    mid = r1.astype(BF16)
    return hi, mid, (r1 - mid.astype(F32)).astype(BF16)


def _sel_dot(sel, x):
    hi, mid, lo = _split3(x)
    s = sel.astype(BF16)
    return (jnp.dot(s, hi, preferred_element_type=F32) + jnp.dot(s, mid, preferred_element_type=F32)
            + jnp.dot(s, lo, preferred_element_type=F32))


def _dot_sel(x, sel):
    hi, mid, lo = _split3(x)
    s = sel.astype(BF16)
    return (jnp.dot(hi, s, preferred_element_type=F32) + jnp.dot(mid, s, preferred_element_type=F32)
            + jnp.dot(lo, s, preferred_element_type=F32))


def _wkv7_kernel(r_ref, k_ref, v_ref, a_ref, lw_ref, g_ref, kk_ref, ka_ref, rk_ref, gnw_ref, gnb_ref,
                 y_ref, z_ref, p_ref, *, heads, clen):
    c = pl.program_id(2)
    row = lax.broadcasted_iota(I32, (clen, clen), 0)
    col = lax.broadcasted_iota(I32, (clen, clen), 1)
    strict = row > col
    incl = row >= col
    tri = incl.astype(F32)
    eye = (row == col).astype(F32)

    @pl.when(c == 0)
    def _():
        z_ref[...] = jnp.zeros_like(z_ref)
        p_ref[...] = jnp.ones_like(p_ref)

    hs = range(heads)
    cols = [slice(h * clen, (h + 1) * clen) for h in hs]
    r = [r_ref[0, :, cols[h]] for h in hs]
    k0 = [k_ref[0, :, cols[h]] for h in hs]
    v = [v_ref[0, :, cols[h]] for h in hs]
    ai = [a_ref[0, :, cols[h]] for h in hs]
    lw = [lw_ref[0, :, cols[h]] for h in hs]
    kk = [k0[h] * kk_ref[:, cols[h]] for h in hs]
    kk = [kk[h] * lax.rsqrt(jnp.maximum(jnp.sum(kk[h] * kk[h], axis=-1, keepdims=True), 1e-24)) for h in hs]
    k = [k0[h] * (1.0 + (ai[h] - 1.0) * ka_ref[:, cols[h]]) for h in hs]
    cum = [_sel_dot(tri, lw[h]) for h in hs]
    ecum = [jnp.exp(cum[h]) for h in hs]
    encum = [jnp.exp(-cum[h]) for h in hs]
    rt = [r[h] * ecum[h] for h in hs]
    kt = [k[h] * encum[h] for h in hs]
    bt = [kk[h] * ai[h] * encum[h] for h in hs]
    at = [-kk[h] * jnp.exp(cum[h] - lw[h]) for h in hs]
    x = [jnp.concatenate([at[h], rt[h]], axis=0) for h in hs]
    aab = [_bdot_nt(x[h], bt[h]) for h in hs]
    aak = [_bdot_nt(x[h], kt[h]) for h in hs]
    a_ab = [jnp.where(strict, aab[h][:clen], 0.0) for h in hs]
    a_ak = [jnp.where(strict, aak[h][:clen], 0.0) for h in hs]
    a_rb = [jnp.where(incl, aab[h][clen:], 0.0) for h in hs]
    a_rk = [jnp.where(incl, aak[h][clen:], 0.0) for h in hs]
    xp = a_ab
    t = [eye + a_ab[h] for h in hs]
    n = 2
    while n < clen:
        xp = [_bdot(xp[h], xp[h]) for h in hs]
        t = [t[h] + _bdot(t[h], xp[h]) for h in hs]
        n *= 2
    av = [_bdot(a_ak[h], v[h]) for h in hs]
    uv = [_bdot(t[h], av[h]) for h in hs]
    at2 = [_bdot(t[h], at[h]) for h in hs]
    rp = [rt[h] + _bdot(a_rb[h], at2[h]) for h in hs]
    yv = [_bdot(a_rb[h], uv[h]) + _bdot(a_rk[h], v[h]) for h in hs]
    mp = [_bdot_tn(bt[h], at2[h]) for h in hs]
    gp = [_bdot_tn(jnp.concatenate([bt[h], kt[h]], axis=0), jnp.concatenate([uv[h], v[h]], axis=0)) for h in hs]
    lhs = [jnp.concatenate([rp[h], eye + mp[h]], axis=0) * p_ref[h] for h in hs]
    out = [_bdot(lhs[h], z_ref[h]) for h in hs]
    for h in hs:
        z_ref[h] = out[h][clen:] + gp[h]
        p_ref[h] = ecum[h][clen - 1:clen, :]
    y = [out[h][:clen] + yv[h] for h in hs]
    mean = [jnp.mean(y[h], axis=-1, keepdims=True) for h in hs]
    var = [jnp.mean(jnp.square(y[h] - mean[h]), axis=-1, keepdims=True) for h in hs]
    yn = [(y[h] - mean[h]) * lax.rsqrt(var[h] + B_GN_EPS) * gnw_ref[:, cols[h]] + gnb_ref[:, cols[h]] for h in hs]
    bonus = [jnp.sum(r[h] * k[h] * rk_ref[:, cols[h]], axis=-1, keepdims=True) * v[h] for h in hs]
    y_ref[0] = (jnp.concatenate([yn[h] + bonus[h] for h in hs], axis=1) * g_ref[0]).astype(y_ref.dtype)


def wkv7_pallas(proj, rkv_col, a, lw, g, k_k, k_a, r_k, gn_w, gn_b, heads_per_step=32, clen=CHUNK):
    bsz, seq, width = a.shape
    nh = width // clen
    gh = min(heads_per_step, nh)
    bw = gh * clen
    assert width % bw == 0 and rkv_col % bw == 0
    nblk = width // bw

    def col_spec(first_block):
        return pl.BlockSpec((1, clen, bw), lambda bi, hi, ci: (bi, ci, first_block + hi))

    par_spec = pl.BlockSpec((1, bw), lambda bi, hi, ci: (0, hi))
    par = lambda t: t.astype(F32).reshape(1, width)
    r0 = rkv_col // bw
    return pl.pallas_call(
        functools.partial(_wkv7_kernel, heads=gh, clen=clen),
        grid=(bsz, nblk, seq // clen),
        in_specs=[col_spec(r0), col_spec(r0 + nblk), col_spec(r0 + 2 * nblk),
                  col_spec(0), col_spec(0), col_spec(0)] + [par_spec] * 5,
        out_specs=col_spec(0),
        out_shape=jax.ShapeDtypeStruct((bsz, seq, width), BF16),
        scratch_shapes=[pltpu.VMEM((gh, clen, clen), F32), pltpu.VMEM((gh, 1, clen), F32)],
        compiler_params=pltpu.CompilerParams(dimension_semantics=("parallel", "parallel", "arbitrary")),
        name="wkv7",
    )(proj, proj, proj, a, lw, g, par(k_k), par(k_a), par(r_k), par(gn_w), par(gn_b))


def _mm3(x, w, **kw):
    bsz, seq, kdim = x.shape
    kp = _round_up(kdim, 128)
    n = w.shape[1]
    np_ = _round_up(n, 128)
    xb = _pad_axis(x.reshape(bsz * seq, kdim).astype(BF16), 1, kp)
    wb = _pad_axis(_pad_axis(w.astype(BF16), 0, kp), 1, np_)
    tn = kw.pop("tn", 512)
    while np_ % tn:
        tn //= 2
    out = matmul_pallas(xb, wb, tn=tn, **kw)
    return out[:, :n].reshape(bsz, seq, n)


def _rwkv7_time_mix(proj, b_col, w_up, w0, a_up, a0, g_up, k_k, k_a, r_k, gn_w, gn_b):
    lo = b_col + 3 * B_WIDTH
    xw = proj[:, :, lo:lo + B_DECAY_RANK]
    xa = proj[:, :, lo + B_DECAY_RANK:lo + B_DECAY_RANK + B_ICLR_RANK]
    xg = proj[:, :, lo + B_DECAY_RANK + B_ICLR_RANK:lo + B_DECAY_RANK + B_ICLR_RANK + B_GATE_RANK]
    w_log = -jax.nn.softplus(-(w0 + _mm3(jnp.tanh(xw), w_up))) - 0.5
    a = jax.nn.sigmoid(a0 + _mm3(xa, a_up))
    g = _mm3(jax.nn.sigmoid(xg), g_up)
    return wkv7_pallas(proj, b_col, a, -jnp.exp(w_log), g, k_k, k_a, r_k, gn_w, gn_b)


CONV_TAIL = 8


def _ssd_kernel(x_ref, bm_ref, cm_ref, z_ref, wx_ref, wb_ref, wc_ref, bx_ref, bb_ref, bc_ref,
                dt_ref, dtt_ref, da_ref, dat_ref, dskip_ref, gn_ref, o_ref,
                state_ref, tx_ref, tb_ref, tc_ref, *, tb, hpg, hdim):
    step = pl.program_id(2)
    width = hpg * hdim

    @pl.when(step == 0)
    def _():
        state_ref[...] = jnp.zeros_like(state_ref)
        tx_ref[...] = jnp.zeros_like(tx_ref)
        tb_ref[...] = jnp.zeros_like(tb_ref)
        tc_ref[...] = jnp.zeros_like(tc_ref)

    def conv_silu(cur_ref, tail_ref, w_ref, b_ref):
        cur = cur_ref[...]
        ext = jnp.concatenate([tail_ref[...], cur], axis=0)
        w = w_ref[...]
        acc = b_ref[...]
        for j in range(C_CONV):
            off = CONV_TAIL - (C_CONV - 1) + j
            acc = acc + w[j:j + 1, :] * ext[off:off + tb, :]
        tail_ref[...] = cur[tb - CONV_TAIL:, :]
        return acc * jax.nn.sigmoid(acc)

    xs = conv_silu(x_ref, tx_ref, wx_ref, bx_ref)
    bm = conv_silu(bm_ref, tb_ref, wb_ref, bb_ref)
    cm = conv_silu(cm_ref, tc_ref, wc_ref, bc_ref)

    row = lax.broadcasted_iota(I32, (CHUNK, CHUNK), 0)
    col = lax.broadcasted_iota(I32, (CHUNK, CHUNK), 1)
    causal = row >= col
    tri = causal.astype(F32)
    triu = (row <= col).astype(F32)
    eh = (lax.broadcasted_iota(I32, (hpg, width), 1) // hdim == lax.broadcasted_iota(I32, (hpg, width), 0)).astype(F32)
    lane_in_pair = lax.broadcasted_iota(I32, (CHUNK, 2 * hdim), 1)

    ys = []
    for c in range(tb // CHUNK):
        lo = c * CHUNK
        xc = xs[lo:lo + CHUNK]
        bc = bm[lo:lo + CHUNK]
        cc = cm[lo:lo + CHUNK]
        dt = dt_ref[0, 0, lo:lo + CHUNK, :]
        dtt = dtt_ref[0, 0, :, lo:lo + CHUNK]
        cum = _sel_dot(tri, da_ref[0, 0, lo:lo + CHUNK, :])
        cumt = _dot_sel(dat_ref[0, 0, :, lo:lo + CHUNK], triu)
        cb = _bdot_nt(cc, bc)
        pairs = []
        for hp in range(hpg // 2):
            xp = xc[:, hp * 2 * hdim:(hp + 1) * 2 * hdim]
            outs = []
            for h in (2 * hp, 2 * hp + 1):
                seg = cum[:, h:h + 1] - cumt[h:h + 1, :]
                wts = cb * jnp.exp(jnp.where(causal, seg, -jnp.inf)) * dtt[h:h + 1, :]
                outs.append(_bdot(wts, xp))
            pairs.append(jnp.where(lane_in_pair < hdim, outs[0], outs[1]))
        y = jnp.concatenate(pairs, axis=1)
        ecum = jnp.exp(cum)
        y = y + _bdot(cc, state_ref[...]) * _dot_sel(ecum, eh)
        last = cum[CHUNK - 1:CHUNK, :]
        to_end = jnp.exp(last - cum) * dt
        xw = xc * _dot_sel(to_end, eh)
        sdec = _dot_sel(jnp.exp(last), eh)
        state_ref[...] = state_ref[...] * sdec + _bdot_tn(bc, xw)
        ys.append(y)
    y = jnp.concatenate(ys, axis=0)
    y = y + dskip_ref[...] * xs
    z = z_ref[...].astype(F32)
    y = y * (z * jax.nn.sigmoid(z))
    y = y * lax.rsqrt(jnp.mean(y * y, axis=-1, keepdims=True) + NORM_EPS) * gn_ref[...]
    o_ref[...] = y.astype(o_ref.dtype)


def ssd_mixer_pallas(z, xbc, dt_raw, conv_w, conv_b, dt_bias, a_log, d_skip, gate_norm, bsz, seq, tb=256):
    m, c_inner = z.shape
    nh = dt_raw.shape[1]
    hpg = nh // C_GROUPS
    width = c_inner // C_GROUPS
    assert width == hpg * C_HEAD_DIM and width % LANES == 0 and C_STATE % LANES == 0 and hpg % 2 == 0
    steps = seq // tb
    dt = jax.nn.softplus(dt_raw.astype(F32) + dt_bias.astype(F32)).reshape(bsz, seq, C_GROUPS, hpg)
    da = dt * -jnp.exp(a_log.astype(F32)).reshape(C_GROUPS, hpg)
    dt_g = jnp.transpose(dt, (0, 2, 1, 3))
    da_g = jnp.transpose(da, (0, 2, 1, 3))
    dt_t = jnp.transpose(dt, (0, 2, 3, 1))
    da_t = jnp.transpose(da, (0, 2, 3, 1))
    dskip = jnp.repeat(d_skip.astype(F32), C_HEAD_DIM).reshape(C_GROUPS, 1, width)
    gn = gate_norm.astype(F32).reshape(C_GROUPS, 1, width)
    cw = conv_w.astype(F32)
    cbias = conv_b.astype(F32).reshape(1, -1)
    boff = c_inner // C_STATE
    coff = boff + C_GROUPS

    row_ix = lambda b, s: b * steps + s
    small = pl.BlockSpec((1, 1, tb, hpg), lambda b, g, s: (b, g, s, 0))
    small_t = pl.BlockSpec((1, 1, hpg, tb), lambda b, g, s: (b, g, 0, s))
    per_group = pl.BlockSpec((None, 1, width), lambda b, g, s: (g, 0, 0))
    in_specs = [
        pl.BlockSpec((tb, width), lambda b, g, s: (row_ix(b, s), g)),
        pl.BlockSpec((tb, C_STATE), lambda b, g, s: (row_ix(b, s), boff + g)),
        pl.BlockSpec((tb, C_STATE), lambda b, g, s: (row_ix(b, s), coff + g)),
        pl.BlockSpec((tb, width), lambda b, g, s: (row_ix(b, s), g)),
        pl.BlockSpec((C_CONV, width), lambda b, g, s: (0, g)),
        pl.BlockSpec((C_CONV, C_STATE), lambda b, g, s: (0, boff + g)),
        pl.BlockSpec((C_CONV, C_STATE), lambda b, g, s: (0, coff + g)),
        pl.BlockSpec((1, width), lambda b, g, s: (0, g)),
        pl.BlockSpec((1, C_STATE), lambda b, g, s: (0, boff + g)),
        pl.BlockSpec((1, C_STATE), lambda b, g, s: (0, coff + g)),
        small, small_t, small, small_t, per_group, per_group,
    ]
    return pl.pallas_call(
        functools.partial(_ssd_kernel, tb=tb, hpg=hpg, hdim=C_HEAD_DIM),
        grid=(bsz, C_GROUPS, steps),
        in_specs=in_specs,
        out_specs=pl.BlockSpec((tb, width), lambda b, g, s: (row_ix(b, s), g)),
        out_shape=jax.ShapeDtypeStruct((m, c_inner), BF16),
        scratch_shapes=[pltpu.VMEM((C_STATE, width), F32), pltpu.VMEM((CONV_TAIL, width), F32),
                        pltpu.VMEM((CONV_TAIL, C_STATE), F32), pltpu.VMEM((CONV_TAIL, C_STATE), F32)],
        compiler_params=pltpu.CompilerParams(
            dimension_semantics=("parallel", "parallel", "arbitrary"), vmem_limit_bytes=VMEM_LIMIT_BYTES),
        name="ssd",
    )(xbc, xbc, xbc, z, cw, cw, cw, cbias, cbias, cbias, dt_g, dt_t, da_g, da_t, dskip, gn)


def _dsa_rwkv_layer(x, positions, norm_mix, w_in, q_norm, w_uq, w_qi, kv_norm, w_uk, w_uv,
                    shift_mu, w_up, w0, a_up, a0, g_up, k_k, k_a, r_k, gn_w, gn_b, w_out,
                    norm_ffn, ffn_gate, ffn_up, ffn_down):
    bsz, seq, d = x.shape
    m = bsz * seq
    x2 = x.reshape(m, d)
    cos_a, sin_a = _rope_tables(positions, A_HEAD_DIM)
    cos_i, sin_i = _rope_tables(positions, IDX_DIM)

    b_in = w_in.shape[1] - A_IN
    b_col = _round_up(A_IN, 8 * B_HEAD_DIM)
    width = _round_up(b_col + b_in, 256)
    w_in_b = w_in.astype(BF16)
    w_in_p = jnp.concatenate([_pad_axis(w_in_b[:, :A_IN], 1, b_col), _pad_axis(w_in_b[:, A_IN:], 1, width - b_col)], axis=1)
    mu_p = jnp.concatenate([jnp.zeros((b_col,), F32), _pad_axis(shift_mu.astype(F32), 0, width - b_col)])
    proj = matmul_shift_pallas(rmsnorm_pallas(x2, norm_mix), w_in_p, mu_p, seq, name="l0_in").reshape(bsz, seq, width)
    c_q = proj[:, :, :A_Q_RANK]
    c_kv = proj[:, :, A_Q_RANK:A_Q_RANK + A_KV_RANK]
    k_idx = proj[:, :, A_Q_RANK + A_KV_RANK:A_Q_RANK + A_KV_RANK + IDX_DIM]
    w_idx = proj[:, :, A_Q_RANK + A_KV_RANK + IDX_DIM:A_IN]
    c_q = _rmsnorm_jax(c_q, q_norm)
    c_kv = _rmsnorm_jax(c_kv, kv_norm)
    c_q2 = c_q.reshape(m, A_Q_RANK).astype(BF16)
    rope_cs = lambda cos, sin: (jnp.concatenate([cos, cos], -1).reshape(m, -1), jnp.concatenate([-sin, sin], -1).reshape(m, -1))
    q = matmul_rope_pallas(c_q2, w_uq.astype(BF16), *rope_cs(cos_a, sin_a), scale=A_HEAD_DIM ** -0.5, name="l0_q")
    q_idx = matmul_rope_pallas(c_q2, w_qi.astype(BF16), *rope_cs(cos_i, sin_i), name="l0_qidx")
    kv = _mm3(c_kv, jnp.concatenate([w_uk, w_uv], axis=1))
    k = _apply_rope(kv[:, :, None, :A_HEAD_DIM], cos_a, sin_a)[:, :, 0]
    v = kv[:, :, A_HEAD_DIM:]
    k_idx = _apply_rope(k_idx[:, :, None, :], cos_i, sin_i)[:, :, 0]
    w_idx = w_idx * (IDX_HEADS * IDX_DIM) ** -0.5
    y_a = dsa_pallas(q.reshape(bsz, seq, -1), q_idx.reshape(bsz, seq, -1), w_idx, k, v, k_idx,
                     topk=min(IDX_TOPK_MAX, seq // 4), ha=A_HEADS, hi=IDX_HEADS)
    y_b = _rwkv7_time_mix(proj, b_col, w_up, w0, a_up, a0, g_up, k_k, k_a, r_k, gn_w, gn_b)
    y = jnp.concatenate([y_a.astype(BF16), y_b], axis=-1).reshape(m, -1)
    x2 = matmul_pallas(y, w_out.astype(BF16), res=x2, name="l0_out")

    ffn = ffn_gate.shape[1]
    ffn_p = _round_up(ffn, 512)
    h = rmsnorm_pallas(x2, norm_ffn)
    wg = _pad_axis(ffn_gate.astype(BF16), 1, ffn_p)[None]
    wu = _pad_axis(ffn_up.astype(BF16), 1, ffn_p)[None]
    wd = _pad_axis(ffn_down.astype(BF16), 0, ffn_p)
    tm = min(1024, m)
    hh = grouped_glu_pallas(h, wg, wu, jnp.zeros((m // tm,), I32), jnp.full((1,), m // tm, I32), tm=tm, name="l0_glu")
    x2 = matmul_pallas(hh, wd, res=x2, tm=1024, tn=1024, tk=ffn_p // 4, name="l0_down")
    return x2.reshape(bsz, seq, d)


def _ssd_moe_layer(x, norm_mix, w_in, conv_w, conv_b, dt_bias, a_log, d_skip, gate_norm, w_out,
                   norm_ffn, router, exp_gate, exp_up, exp_down):
    bsz, seq, d = x.shape
    m = bsz * seq
    x2 = x.reshape(m, d)
    c_inner = w_out.shape[0]
    c_conv_dim = c_inner + 2 * C_GROUPS * C_STATE
    h = rmsnorm_pallas(x2, norm_mix)
    w_in_b = w_in.astype(BF16)
    z = matmul_pallas(h, w_in_b[:, :c_inner], name="l1_in_z")
    xbc = matmul_pallas(h, w_in_b[:, c_inner:c_inner + c_conv_dim], name="l1_in_xbc")
    dt_raw = matmul_pallas(h, w_in_b[:, c_inner + c_conv_dim:], name="l1_in_dt")
    y = ssd_mixer_pallas(z, xbc, dt_raw, conv_w, conv_b, dt_bias, a_log, d_skip, gate_norm, bsz, seq)
    x2 = matmul_pallas(y, w_out.astype(BF16), res=x2, tm=1024, tn=1024, tk=2048, name="l1_out")
    x2 = moe_sparse(x2, norm_ffn, router, exp_gate, exp_up, exp_down)
    return x2.reshape(bsz, seq, d)


def kernel(x, positions, l0_norm_mix, l0_w_in, l0_q_norm, l0_w_uq, l0_w_qi, l0_kv_norm, l0_w_uk, l0_w_uv,
           l0_shift_mu, l0_w_up, l0_w0, l0_a_up, l0_a0, l0_g_up, l0_k_k, l0_k_a, l0_r_k, l0_gn_w, l0_gn_b,
           l0_w_out, l0_norm_ffn, l0_ffn_gate, l0_ffn_up, l0_ffn_down,
           l1_norm_mix, l1_w_in, l1_conv_w, l1_conv_b, l1_dt_bias, l1_a_log, l1_d_skip, l1_gate_norm, l1_w_out,
           l1_norm_ffn, l1_router, l1_exp_gate, l1_exp_up, l1_exp_down, final_norm):
    x = _dsa_rwkv_layer(x, positions, l0_norm_mix, l0_w_in, l0_q_norm, l0_w_uq, l0_w_qi, l0_kv_norm, l0_w_uk,
                        l0_w_uv, l0_shift_mu, l0_w_up, l0_w0, l0_a_up, l0_a0, l0_g_up, l0_k_k, l0_k_a, l0_r_k,
                        l0_gn_w, l0_gn_b, l0_w_out, l0_norm_ffn, l0_ffn_gate, l0_ffn_up, l0_ffn_down)
    x = _ssd_moe_layer(x, l1_norm_mix, l1_w_in, l1_conv_w, l1_conv_b, l1_dt_bias, l1_a_log, l1_d_skip,
                       l1_gate_norm, l1_w_out, l1_norm_ffn, l1_router, l1_exp_gate, l1_exp_up, l1_exp_down)
    bsz, seq, d = x.shape
    return rmsnorm_pallas(x.reshape(bsz * seq, d), final_norm, out_dtype=x.dtype).reshape(bsz, seq, d)
```

```python
import functools

import numpy as np
import jax
import jax.numpy as jnp
from jax import lax
from jax.experimental import pallas as pl
from jax.experimental.pallas import tpu as pltpu

F32 = jnp.float32
BF16 = jnp.bfloat16
I32 = jnp.int32
HI = lax.Precision.HIGHEST

CHUNK = 64
QBLOCK = 128
ROPE_THETA = 10000.0
NORM_EPS = 1e-6

A_HEADS = 16
A_HEAD_DIM = 128
A_Q_RANK = 1024
A_KV_RANK = 512
IDX_HEADS = 32
IDX_DIM = 128
IDX_TOPK_MAX = 256

B_HEADS = 32
B_HEAD_DIM = 64
B_WIDTH = B_HEADS * B_HEAD_DIM
B_DECAY_RANK = 128
B_ICLR_RANK = 128
B_GATE_RANK = 480
B_GN_EPS = 64e-5
A_IN = A_Q_RANK + A_KV_RANK + IDX_DIM + IDX_HEADS

C_HEAD_DIM = 64
C_GROUPS = 8
C_STATE = 128
C_CONV = 4

TOP_K_EXPERTS = 2

LANES = 128
VMEM_LIMIT_BYTES = 56 * 1024 * 1024


def _round_up(n, m):
    return (n + m - 1) // m * m


def _pad_axis(a, axis, size):
    if a.shape[axis] == size:
        return a
    pads = [(0, 0)] * a.ndim
    pads[axis] = (0, size - a.shape[axis])
    return jnp.pad(a, pads)


def _rmsnorm_kernel(x_ref, g_ref, o_ref):
    x = x_ref[...].astype(F32)
    ms = jnp.mean(x * x, axis=-1, keepdims=True)
    o_ref[...] = (x * lax.rsqrt(ms + NORM_EPS) * g_ref[...]).astype(o_ref.dtype)


def rmsnorm_pallas(x, g, out_dtype=BF16, tm=256):
    m, d = x.shape
    tm = min(tm, m)
    return pl.pallas_call(
        _rmsnorm_kernel,
        grid=(m // tm,),
        in_specs=[pl.BlockSpec((tm, d), lambda i: (i, 0)),
                  pl.BlockSpec((1, d), lambda i: (0, 0))],
        out_specs=pl.BlockSpec((tm, d), lambda i: (i, 0)),
        out_shape=jax.ShapeDtypeStruct((m, d), out_dtype),
        compiler_params=pltpu.CompilerParams(dimension_semantics=("parallel",)),
        name="rmsnorm",
    )(x, g.reshape(1, d).astype(F32))


def _mm_kernel(x_ref, w_ref, *rest, nk, has_res):
    if has_res:
        res_ref, o_ref, acc_ref = rest
    else:
        o_ref, acc_ref = rest
        res_ref = None
    p = jnp.dot(x_ref[...], w_ref[...], preferred_element_type=F32)

    def finish(acc):
        if has_res:
            acc = acc + res_ref[...].astype(F32)
        o_ref[...] = acc.astype(o_ref.dtype)

    if nk == 1:
        finish(p)
        return
    k = pl.program_id(2)

    @pl.when(k == 0)
    def _():
        acc_ref[...] = p

    @pl.when(jnp.logical_and(k > 0, k < nk - 1))
    def _():
        acc_ref[...] += p

    @pl.when(k == nk - 1)
    def _():
        finish(acc_ref[...] + p)


def matmul_pallas(x, w, res=None, out_dtype=F32, tm=1024, tn=512, tk=None, name="matmul"):
    m, kdim = x.shape
    k2, n = w.shape
    assert kdim == k2
    tm = min(tm, m)
    tn = min(tn, n)
    tk = kdim if tk is None else min(tk, kdim)
    assert m % tm == 0 and n % tn == 0 and kdim % tk == 0, (x.shape, w.shape, tm, tn, tk)
    nk = kdim // tk
    in_specs = [pl.BlockSpec((tm, tk), lambda i, j, k: (i, k)),
                pl.BlockSpec((tk, tn), lambda i, j, k: (k, j))]
    args = [x, w]
    if res is not None:
        in_specs.append(pl.BlockSpec((tm, tn), lambda i, j, k: (i, j)))
        args.append(res)
    acc_shape = (tm, tn) if nk > 1 else (8, 128)
    return pl.pallas_call(
        functools.partial(_mm_kernel, nk=nk, has_res=res is not None),
        grid=(m // tm, n // tn, nk),
        in_specs=in_specs,
        out_specs=pl.BlockSpec((tm, tn), lambda i, j, k: (i, j)),
        out_shape=jax.ShapeDtypeStruct((m, n), out_dtype),
        scratch_shapes=[pltpu.VMEM(acc_shape, F32)],
        compiler_params=pltpu.CompilerParams(
            dimension_semantics=("parallel", "parallel", "arbitrary"),
            vmem_limit_bytes=VMEM_LIMIT_BYTES),
        name=name,
    )(*args)


def _mm_rope_kernel(x_ref, w_ref, cos_ref, sin_ref, o_ref, *, scale, head_dim):
    p = jnp.dot(x_ref[...], w_ref[...], preferred_element_type=F32) * scale
    cos = cos_ref[...]
    sin = sin_ref[...]
    heads = []
    for h in range(p.shape[1] // head_dim):
        ph = p[:, h * head_dim:(h + 1) * head_dim]
        heads.append(ph * cos + pltpu.roll(ph, head_dim // 2, 1) * sin)
    o_ref[...] = jnp.concatenate(heads, axis=1).astype(o_ref.dtype)


def matmul_rope_pallas(x, w, cos_full, sin_signed, scale=1.0, tm=1024, tn=512, name="matmul_rope"):
    m, kdim = x.shape
    n = w.shape[1]
    hd = cos_full.shape[1]
    tm = min(tm, m)
    assert m % tm == 0 and n % tn == 0 and tn % hd == 0
    return pl.pallas_call(
        functools.partial(_mm_rope_kernel, scale=scale, head_dim=hd),
        grid=(m // tm, n // tn),
        in_specs=[pl.BlockSpec((tm, kdim), lambda i, j: (i, 0)),
                  pl.BlockSpec((kdim, tn), lambda i, j: (0, j)),
                  pl.BlockSpec((tm, hd), lambda i, j: (i, 0)),
                  pl.BlockSpec((tm, hd), lambda i, j: (i, 0))],
        out_specs=pl.BlockSpec((tm, tn), lambda i, j: (i, j)),
        out_shape=jax.ShapeDtypeStruct((m, n), BF16),
        compiler_params=pltpu.CompilerParams(
            dimension_semantics=("parallel", "parallel"), vmem_limit_bytes=VMEM_LIMIT_BYTES),
        name=name,
    )(x, w, cos_full, sin_signed)


SHIFT_TAIL = 8


def _mm_shift_kernel(x_ref, w_ref, mu_ref, o_ref, carry_ref, *, tiles_per_seq):
    i = pl.program_id(0)
    j = pl.program_id(1)

    @pl.when(i == 0)
    def _():
        carry_ref[j] = jnp.zeros(carry_ref.shape[1:], F32)

    p = jnp.dot(x_ref[...], w_ref[...], preferred_element_type=F32)
    tm = p.shape[0]
    tail = jnp.where(i % tiles_per_seq == 0, 0.0, carry_ref[j])
    prev = jnp.concatenate([tail, p], axis=0)[SHIFT_TAIL - 1:SHIFT_TAIL - 1 + tm]
    carry_ref[j] = p[tm - SHIFT_TAIL:]
    o_ref[...] = p + (prev - p) * mu_ref[...]


def matmul_shift_pallas(x, w, mu, seq, tm=1024, tn=256, name="matmul_shift"):
    m, kdim = x.shape
    n = w.shape[1]
    tm = min(tm, seq)
    assert seq % tm == 0 and m % seq == 0 and n % tn == 0
    return pl.pallas_call(
        functools.partial(_mm_shift_kernel, tiles_per_seq=seq // tm),
        grid=(m // tm, n // tn),
        in_specs=[pl.BlockSpec((tm, kdim), lambda i, j: (i, 0)),
                  pl.BlockSpec((kdim, tn), lambda i, j: (0, j)),
                  pl.BlockSpec((1, tn), lambda i, j: (0, j))],
        out_specs=pl.BlockSpec((tm, tn), lambda i, j: (i, j)),
        out_shape=jax.ShapeDtypeStruct((m, n), F32),
        scratch_shapes=[pltpu.VMEM((n // tn, SHIFT_TAIL, tn), F32)],
        compiler_params=pltpu.CompilerParams(
            dimension_semantics=("arbitrary", "arbitrary"), vmem_limit_bytes=VMEM_LIMIT_BYTES),
        name=name,
    )(x, w, mu.reshape(1, n).astype(F32))


def _gglu_kernel(te_ref, nu_ref, x_ref, wg_ref, wu_ref, o_ref):
    t = pl.program_id(0)

    @pl.when(t < nu_ref[0])
    def _():
        x = x_ref[...]
        g = jnp.dot(x, wg_ref[...], preferred_element_type=F32)
        u = jnp.dot(x, wu_ref[...], preferred_element_type=F32)
        o_ref[...] = (g * jax.nn.sigmoid(g) * u).astype(o_ref.dtype)

    @pl.when(t >= nu_ref[0])
    def _():
        o_ref[...] = jnp.zeros_like(o_ref)


def grouped_glu_pallas(xs, wg, wu, tile_expert, n_used, tm, tn=512, name="glu"):
    r, kdim = xs.shape
    n = wg.shape[2]
    assert r % tm == 0 and n % tn == 0
    grid_spec = pltpu.PrefetchScalarGridSpec(
        num_scalar_prefetch=2,
        grid=(r // tm, n // tn),
        in_specs=[pl.BlockSpec((tm, kdim), lambda t, j, te, nu: (t, 0)),
                  pl.BlockSpec((None, kdim, tn), lambda t, j, te, nu: (te[t], 0, j)),
                  pl.BlockSpec((None, kdim, tn), lambda t, j, te, nu: (te[t], 0, j))],
        out_specs=pl.BlockSpec((tm, tn), lambda t, j, te, nu: (t, j)),
    )
    return pl.pallas_call(
        _gglu_kernel,
        grid_spec=grid_spec,
        out_shape=jax.ShapeDtypeStruct((r, n), BF16),
        compiler_params=pltpu.CompilerParams(
            dimension_semantics=("arbitrary", "arbitrary"), vmem_limit_bytes=VMEM_LIMIT_BYTES),
        name=name,
    )(tile_expert, n_used, xs, wg, wu)


def _gdown_kernel(te_ref, nu_ref, x_ref, w_ref, o_ref):
    t = pl.program_id(0)

    @pl.when(t < nu_ref[0])
    def _():
        o_ref[...] = jnp.dot(x_ref[...], w_ref[...], preferred_element_type=F32).astype(o_ref.dtype)

    @pl.when(t >= nu_ref[0])
    def _():
        o_ref[...] = jnp.zeros_like(o_ref)


def grouped_down_pallas(hs, wd, tile_expert, n_used, tm, tn=512, name="down"):
    r, kdim = hs.shape
    n = wd.shape[2]
    assert r % tm == 0 and n % tn == 0
    grid_spec = pltpu.PrefetchScalarGridSpec(
        num_scalar_prefetch=2,
        grid=(r // tm, n // tn),
        in_specs=[pl.BlockSpec((tm, kdim), lambda t, j, te, nu: (t, 0)),
                  pl.BlockSpec((None, kdim, tn), lambda t, j, te, nu: (te[t], 0, j))],
        out_specs=pl.BlockSpec((tm, tn), lambda t, j, te, nu: (t, j)),
    )
    return pl.pallas_call(
        _gdown_kernel,
        grid_spec=grid_spec,
        out_shape=jax.ShapeDtypeStruct((r, n), BF16),
        compiler_params=pltpu.CompilerParams(
            dimension_semantics=("arbitrary", "arbitrary"), vmem_limit_bytes=VMEM_LIMIT_BYTES),
        name=name,
    )(tile_expert, n_used, hs, wd)


def _router_kernel(x_ref, g_ref, wr_ref, h_ref, r_ref, *, n_experts):
    x = x_ref[...]
    h = x * lax.rsqrt(jnp.mean(x * x, axis=-1, keepdims=True) + NORM_EPS) * g_ref[...]
    h_ref[...] = h.astype(h_ref.dtype)
    logits = jnp.dot(h, wr_ref[...], preferred_element_type=F32, precision=HI)
    lane = lax.broadcasted_iota(I32, logits.shape, 1)
    logits = jnp.where(lane < n_experts, logits, -jnp.inf)
    m1 = jnp.max(logits, axis=1, keepdims=True)
    i1 = jnp.min(jnp.where(logits == m1, lane, LANES), axis=1, keepdims=True)
    rest = jnp.where(lane == i1, -jnp.inf, logits)
    m2 = jnp.max(rest, axis=1, keepdims=True)
    i2 = jnp.min(jnp.where(rest == m2, lane, LANES), axis=1, keepdims=True)
    e2 = jnp.exp(m2 - m1)
    w1 = 1.0 / (1.0 + e2)
    w2 = e2 / (1.0 + e2)
    out = jnp.where(lane == 0, i1.astype(F32), 0.0)
    out = jnp.where(lane == 1, i2.astype(F32), out)
    out = jnp.where(lane == 2, w1, out)
    out = jnp.where(lane == 3, w2, out)
    r_ref[...] = out


def router_pallas(x, g, router, tm=256):
    m, d = x.shape
    ne = router.shape[1]
    wr = jnp.pad(router.astype(F32), ((0, 0), (0, LANES - ne)))
    return pl.pallas_call(
        functools.partial(_router_kernel, n_experts=ne),
        grid=(m // tm,),
        in_specs=[pl.BlockSpec((tm, d), lambda i: (i, 0)),
                  pl.BlockSpec((1, d), lambda i: (0, 0)),
                  pl.BlockSpec((d, LANES), lambda i: (0, 0))],
        out_specs=[pl.BlockSpec((tm, d), lambda i: (i, 0)),
                   pl.BlockSpec((tm, LANES), lambda i: (i, 0))],
        out_shape=[jax.ShapeDtypeStruct((m, d), BF16), jax.ShapeDtypeStruct((m, LANES), F32)],
        compiler_params=pltpu.CompilerParams(dimension_semantics=("parallel",)),
        name="router",
    )(x, g.reshape(1, d).astype(F32), wr)


def moe_sparse(x2, norm_ffn, router, exp_gate, exp_up, exp_down, tm=512):
    m, d = x2.shape
    ne = router.shape[1]
    h, route = router_pallas(x2, norm_ffn, router)
    idx = route[:, :TOP_K_EXPERTS].astype(I32)
    flat_e = idx.reshape(-1)
    onehot = (flat_e[:, None] == jnp.arange(ne, dtype=I32)[None, :]).astype(I32)
    rank = jnp.sum((jnp.cumsum(onehot, axis=0) - 1) * onehot, axis=1)
    count = jnp.sum(onehot, axis=0)
    padded = (count + tm - 1) // tm * tm
    ends = jnp.cumsum(padded)
    pos = (ends - padded)[flat_e] + rank
    rows = m * TOP_K_EXPERTS + ne * tm
    n_tiles = rows // tm
    token_of_row = jnp.zeros((rows,), I32).at[pos].set(jnp.arange(m * TOP_K_EXPERTS, dtype=I32) // TOP_K_EXPERTS)
    tile_start = jnp.arange(n_tiles, dtype=I32)[:, None] * tm
    tile_expert = jnp.minimum(jnp.sum((tile_start >= ends[None, :]).astype(I32), axis=1), ne - 1).astype(I32)
    n_used = (ends[-1] // tm).astype(I32).reshape(1)
    xs = jnp.take(h, token_of_row, axis=0)
    hh = grouped_glu_pallas(xs, exp_gate.astype(BF16), exp_up.astype(BF16), tile_expert, n_used, tm, name="moe_glu")
    ys = grouped_down_pallas(hh, exp_down.astype(BF16), tile_expert, n_used, tm, name="moe_down")
    pos2 = pos.reshape(m, TOP_K_EXPERTS)
    return jnp.take(ys, pos2[:, 0], axis=0), jnp.take(ys, pos2[:, 1], axis=0), route


def _combine_norm_kernel(x_ref, y1_ref, y2_ref, r_ref, g_ref, o_ref):
    r = r_ref[...]
    w1 = r[:, TOP_K_EXPERTS:TOP_K_EXPERTS + 1]
    w2 = r[:, TOP_K_EXPERTS + 1:TOP_K_EXPERTS + 2]
    x = x_ref[...] + y1_ref[...].astype(F32) * w1 + y2_ref[...].astype(F32) * w2
    ms = jnp.mean(x * x, axis=-1, keepdims=True)
    o_ref[...] = (x * lax.rsqrt(ms + NORM_EPS) * g_ref[...]).astype(o_ref.dtype)


def combine_norm_pallas(x, y1, y2, route, g, tm=256):
    m, d = x.shape
    row = pl.BlockSpec((tm, d), lambda i: (i, 0))
    return pl.pallas_call(
        _combine_norm_kernel,
        grid=(m // tm,),
        in_specs=[row, row, row, pl.BlockSpec((tm, LANES), lambda i: (i, 0)), pl.BlockSpec((1, d), lambda i: (0, 0))],
        out_specs=row,
        out_shape=jax.ShapeDtypeStruct((m, d), x.dtype),
        compiler_params=pltpu.CompilerParams(dimension_semantics=("parallel",)),
        name="combine_norm",
    )(x, y1, y2, route, g.reshape(1, d).astype(F32))


def _rmsnorm_jax(x, g):
    xf = x.astype(F32)
    return xf * lax.rsqrt(jnp.mean(xf * xf, axis=-1, keepdims=True) + NORM_EPS) * g.astype(F32)


def _rope_tables(positions, dim):
    inv_freq = ROPE_THETA ** (-jnp.arange(0, dim, 2, dtype=F32) / dim)
    ang = positions.astype(F32)[..., None] * inv_freq
    return jnp.cos(ang)[:, :, None, :], jnp.sin(ang)[:, :, None, :]


def _apply_rope(x, cos, sin):
    x1, x2 = jnp.split(x.astype(F32), 2, axis=-1)
    return jnp.concatenate([x1 * cos - x2 * sin, x2 * cos + x1 * sin], axis=-1)


NEG_BIG = -1e30
INT_MIN = np.int32(-2 ** 31)


def _dsa_kernel(qi_ref, w_ref, q_ref, kit_ref, kt_ref, v_ref, o_ref,
                s_ref, wb_ref, m_ref, l_ref, acc_ref, qis_ref, qs_ref, *, tk, idx_heads, att_heads, topk, head_group):
    i = pl.program_id(1)
    hd_i = qi_ref.shape[2] // idx_heads
    for h in range(idx_heads):
        qis_ref[h * QBLOCK:(h + 1) * QBLOCK, :] = qi_ref[0, :, h * hd_i:(h + 1) * hd_i]
    hd_a = q_ref.shape[2] // att_heads
    for h in range(att_heads):
        qs_ref[h * QBLOCK:(h + 1) * QBLOCK, :] = q_ref[0, :, h * hd_a:(h + 1) * hd_a]
    n_tiles = ((i + 1) * QBLOCK + tk - 1) // tk
    nl = tk // LANES
    row = lax.broadcasted_iota(I32, (QBLOCK, tk), 0)
    col = lax.broadcasted_iota(I32, (QBLOCK, tk), 1)
    limit = (2 * i + 1 + (row >= CHUNK).astype(I32)) * CHUNK

    def admissible(t):
        return (t * tk + col) < limit

    def fold_lanes(x, op):
        out = x[:, :LANES]
        for j in range(1, nl):
            out = op(out, x[:, j * LANES:(j + 1) * LANES])
        return out

    w = w_ref[0, 0]
    for h in range(idx_heads):
        wb_ref[h] = jnp.broadcast_to(w[:, h:h + 1], (QBLOCK, LANES))

    def idx_body(t, carry):
        kit = kit_ref[0, t]
        acc = jnp.zeros((QBLOCK, tk), F32)
        for g in range(idx_heads // head_group):
            rows = head_group * QBLOCK
            rel = jnp.dot(qis_ref[g * rows:(g + 1) * rows, :], kit, preferred_element_type=F32)
            for hh in range(head_group):
                h = g * head_group + hh
                relu = jnp.maximum(rel[hh * QBLOCK:(hh + 1) * QBLOCK], 0.0)
                acc = acc + jnp.concatenate([wb_ref[h]] * nl, axis=1) * relu
        score = jnp.where(admissible(t), acc, -jnp.inf)
        bits = lax.bitcast_convert_type(score, I32)
        s_ref[t] = bits ^ ((bits >> 31) & np.int32(0x7FFFFFFF))
        return carry

    lax.fori_loop(0, n_tiles, idx_body, 0)

    def bit_body(it, ans):
        cand = ans | lax.shift_left(np.int32(1), 31 - it)
        cand_s = cand ^ INT_MIN

        def cnt_body(t, cnt):
            return cnt + fold_lanes((s_ref[t] >= cand_s).astype(I32), jnp.add)

        cnt = lax.fori_loop(0, n_tiles, cnt_body, jnp.zeros((QBLOCK, LANES), I32))
        total = jnp.sum(cnt, axis=1, keepdims=True)
        return jnp.where(total >= topk, cand, ans)

    ans = lax.fori_loop(0, 32, bit_body, jnp.zeros((QBLOCK, 1), I32))
    thr = ans ^ INT_MIN

    def masked_logits(t):
        sel = jnp.logical_and(s_ref[t] >= thr, admissible(t))
        bias = jnp.where(sel, 0.0, NEG_BIG)
        logits = jnp.dot(qs_ref[...], kt_ref[0, t], preferred_element_type=F32)
        return logits + jnp.concatenate([bias] * att_heads, axis=0)

    m_ref[...] = jnp.full_like(m_ref, NEG_BIG)

    def max_body(t, carry):
        m_ref[...] = jnp.maximum(m_ref[...], fold_lanes(masked_logits(t), jnp.maximum))
        return carry

    lax.fori_loop(0, n_tiles, max_body, 0)
    m_ref[...] = jnp.broadcast_to(jnp.max(m_ref[...], axis=1, keepdims=True), m_ref.shape)

    l_ref[...] = jnp.zeros_like(l_ref)
    acc_ref[...] = jnp.zeros_like(acc_ref)

    def att_body(t, carry):
        p = jnp.exp(masked_logits(t) - jnp.concatenate([m_ref[...]] * nl, axis=1))
        l_ref[...] += fold_lanes(p, jnp.add)
        acc_ref[...] += jnp.dot(p.astype(BF16), v_ref[0, t], preferred_element_type=F32)
        return carry

    lax.fori_loop(0, n_tiles, att_body, 0)
    out = acc_ref[...] / jnp.sum(l_ref[...], axis=1, keepdims=True)
    d = out.shape[1]
    for h in range(att_heads):
        o_ref[0, :, h * d:(h + 1) * d] = out[h * QBLOCK:(h + 1) * QBLOCK, :]


def dsa_pallas(q, q_idx, w_idx, k, v, k_idx, topk, ha, hi, tk=512, head_group=8):
    bsz, seq, _ = q.shape
    d = q.shape[2] // ha
    di = q_idx.shape[2] // hi
    nb = seq // QBLOCK
    nt = seq // tk
    wb = w_idx.astype(F32).reshape(bsz, nb, QBLOCK, hi)
    kit = jnp.transpose(k_idx.astype(BF16).reshape(bsz, nt, tk, di), (0, 1, 3, 2))
    kt = jnp.transpose(k.astype(BF16).reshape(bsz, nt, tk, d), (0, 1, 3, 2))
    vb = v.astype(BF16).reshape(bsz, nt, tk, d)
    return pl.pallas_call(
        functools.partial(_dsa_kernel, tk=tk, idx_heads=hi, att_heads=ha, topk=topk, head_group=head_group),
        grid=(bsz, nb),
        in_specs=[
            pl.BlockSpec((1, QBLOCK, hi * di), lambda b, i: (b, i, 0)),
            pl.BlockSpec((1, 1, QBLOCK, hi), lambda b, i: (b, i, 0, 0)),
            pl.BlockSpec((1, QBLOCK, ha * d), lambda b, i: (b, i, 0)),
            pl.BlockSpec((1, nt, di, tk), lambda b, i: (b, 0, 0, 0)),
            pl.BlockSpec((1, nt, d, tk), lambda b, i: (b, 0, 0, 0)),
            pl.BlockSpec((1, nt, tk, d), lambda b, i: (b, 0, 0, 0)),
        ],
        out_specs=pl.BlockSpec((1, QBLOCK, ha * d), lambda b, i: (b, i, 0)),
        out_shape=jax.ShapeDtypeStruct((bsz, seq, ha * d), F32),
        scratch_shapes=[
            pltpu.VMEM((nt, QBLOCK, tk), I32),
            pltpu.VMEM((hi, QBLOCK, LANES), F32),
            pltpu.VMEM((ha * QBLOCK, LANES), F32),
            pltpu.VMEM((ha * QBLOCK, LANES), F32),
            pltpu.VMEM((ha * QBLOCK, d), F32),
            pltpu.VMEM((hi * QBLOCK, di), BF16),
            pltpu.VMEM((ha * QBLOCK, d), BF16),
        ],
        compiler_params=pltpu.CompilerParams(
            dimension_semantics=("parallel", "arbitrary"),
            vmem_limit_bytes=VMEM_LIMIT_BYTES),
        name="dsa",
    )(q_idx, wb, q, kit, kt, vb)


def _bdot(a, b):
    return jnp.dot(a.astype(BF16), b.astype(BF16), preferred_element_type=F32)


def _bdot_nt(a, b):
    return lax.dot_general(a.astype(BF16), b.astype(BF16), (((1,), (1,)), ((), ())), preferred_element_type=F32)


def _bdot_tn(a, b):
    return lax.dot_general(a.astype(BF16), b.astype(BF16), (((0,), (0,)), ((), ())), preferred_element_type=F32)


def _split3(x):
    hi = x.astype(BF16)
    r1 = x - hi.astype(F32)
    mid = r1.astype(BF16)
    return hi, mid, (r1 - mid.astype(F32)).astype(BF16)


def _sel_dot(sel, x):
    hi, mid, lo = _split3(x)
    s = sel.astype(BF16)
    return (jnp.dot(s, hi, preferred_element_type=F32) + jnp.dot(s, mid, preferred_element_type=F32)
            + jnp.dot(s, lo, preferred_element_type=F32))


def _dot_sel(x, sel):
    hi, mid, lo = _split3(x)
    s = sel.astype(BF16)
    return (jnp.dot(hi, s, preferred_element_type=F32) + jnp.dot(mid, s, preferred_element_type=F32)
            + jnp.dot(lo, s, preferred_element_type=F32))


def _wkv7_kernel(r_ref, k_ref, v_ref, a_ref, lw_ref, g_ref, kk_ref, ka_ref, rk_ref, gnw_ref, gnb_ref,
                 y_ref, z_ref, p_ref, *, heads, clen):
    c = pl.program_id(2)
    row = lax.broadcasted_iota(I32, (clen, clen), 0)
    col = lax.broadcasted_iota(I32, (clen, clen), 1)
    strict = row > col
    incl = row >= col
    tri = incl.astype(F32)
    eye = (row == col).astype(F32)

    @pl.when(c == 0)
    def _():
        z_ref[...] = jnp.zeros_like(z_ref)
        p_ref[...] = jnp.ones_like(p_ref)

    hs = range(heads)
    cols = [slice(h * clen, (h + 1) * clen) for h in hs]
    r = [r_ref[0, :, cols[h]] for h in hs]
    k0 = [k_ref[0, :, cols[h]] for h in hs]
    v = [v_ref[0, :, cols[h]] for h in hs]
    ai = [a_ref[0, :, cols[h]] for h in hs]
    lw = [lw_ref[0, :, cols[h]] for h in hs]
    kk = [k0[h] * kk_ref[:, cols[h]] for h in hs]
    kk = [kk[h] * lax.rsqrt(jnp.maximum(jnp.sum(kk[h] * kk[h], axis=-1, keepdims=True), 1e-24)) for h in hs]
    k = [k0[h] * (1.0 + (ai[h] - 1.0) * ka_ref[:, cols[h]]) for h in hs]
    cum = [_sel_dot(tri, lw[h]) for h in hs]
    ecum = [jnp.exp(cum[h]) for h in hs]
    encum = [jnp.exp(-cum[h]) for h in hs]
    rt = [r[h] * ecum[h] for h in hs]
    kt = [k[h] * encum[h] for h in hs]
    bt = [kk[h] * ai[h] * encum[h] for h in hs]
    at = [-kk[h] * jnp.exp(cum[h] - lw[h]) for h in hs]
    x = [jnp.concatenate([at[h], rt[h]], axis=0) for h in hs]
    aab = [_bdot_nt(x[h], bt[h]) for h in hs]
    aak = [_bdot_nt(x[h], kt[h]) for h in hs]
    a_ab = [jnp.where(strict, aab[h][:clen], 0.0) for h in hs]
    a_ak = [jnp.where(strict, aak[h][:clen], 0.0) for h in hs]
    a_rb = [jnp.where(incl, aab[h][clen:], 0.0) for h in hs]
    a_rk = [jnp.where(incl, aak[h][clen:], 0.0) for h in hs]
    xp = a_ab
    t = [eye + a_ab[h] for h in hs]
    n = 2
    while n < clen:
        xp = [_bdot(xp[h], xp[h]) for h in hs]
        t = [t[h] + _bdot(t[h], xp[h]) for h in hs]
        n *= 2
    av = [_bdot(a_ak[h], v[h]) for h in hs]
    uv = [_bdot(t[h], av[h]) for h in hs]
    at2 = [_bdot(t[h], at[h]) for h in hs]
    rp = [rt[h] + _bdot(a_rb[h], at2[h]) for h in hs]
    yv = [_bdot(a_rb[h], uv[h]) + _bdot(a_rk[h], v[h]) for h in hs]
    mp = [_bdot_tn(bt[h], at2[h]) for h in hs]
    gp = [_bdot_tn(jnp.concatenate([bt[h], kt[h]], axis=0), jnp.concatenate([uv[h], v[h]], axis=0)) for h in hs]
    lhs = [jnp.concatenate([rp[h], eye + mp[h]], axis=0) * p_ref[h] for h in hs]
    out = [_bdot(lhs[h], z_ref[h]) for h in hs]
    for h in hs:
        z_ref[h] = out[h][clen:] + gp[h]
        p_ref[h] = ecum[h][clen - 1:clen, :]
    y = [out[h][:clen] + yv[h] for h in hs]
    mean = [jnp.mean(y[h], axis=-1, keepdims=True) for h in hs]
    var = [jnp.mean(jnp.square(y[h] - mean[h]), axis=-1, keepdims=True) for h in hs]
    yn = [(y[h] - mean[h]) * lax.rsqrt(var[h] + B_GN_EPS) * gnw_ref[:, cols[h]] + gnb_ref[:, cols[h]] for h in hs]
    bonus = [jnp.sum(r[h] * k[h] * rk_ref[:, cols[h]], axis=-1, keepdims=True) * v[h] for h in hs]
    y_ref[0] = (jnp.concatenate([yn[h] + bonus[h] for h in hs], axis=1) * g_ref[0]).astype(y_ref.dtype)


def wkv7_pallas(proj, rkv_col, a, lw, g, k_k, k_a, r_k, gn_w, gn_b, heads_per_step=32, clen=CHUNK):
    bsz, seq, width = a.shape
    nh = width // clen
    gh = min(heads_per_step, nh)
    bw = gh * clen
    assert width % bw == 0 and rkv_col % bw == 0
    nblk = width // bw

    def col_spec(first_block):
        return pl.BlockSpec((1, clen, bw), lambda bi, hi, ci: (bi, ci, first_block + hi))

    par_spec = pl.BlockSpec((1, bw), lambda bi, hi, ci: (0, hi))
    par = lambda t: t.astype(F32).reshape(1, width)
    r0 = rkv_col // bw
    return pl.pallas_call(
        functools.partial(_wkv7_kernel, heads=gh, clen=clen),
        grid=(bsz, nblk, seq // clen),
        in_specs=[col_spec(r0), col_spec(r0 + nblk), col_spec(r0 + 2 * nblk),
                  col_spec(0), col_spec(0), col_spec(0)] + [par_spec] * 5,
        out_specs=col_spec(0),
        out_shape=jax.ShapeDtypeStruct((bsz, seq, width), BF16),
        scratch_shapes=[pltpu.VMEM((gh, clen, clen), F32), pltpu.VMEM((gh, 1, clen), F32)],
        compiler_params=pltpu.CompilerParams(dimension_semantics=("parallel", "parallel", "arbitrary")),
        name="wkv7",
    )(proj, proj, proj, a, lw, g, par(k_k), par(k_a), par(r_k), par(gn_w), par(gn_b))


def _mm3(x, w, **kw):
    bsz, seq, kdim = x.shape
    kp = _round_up(kdim, 128)
    n = w.shape[1]
    np_ = _round_up(n, 128)
    xb = _pad_axis(x.reshape(bsz * seq, kdim).astype(BF16), 1, kp)
    wb = _pad_axis(_pad_axis(w.astype(BF16), 0, kp), 1, np_)
    tn = kw.pop("tn", 512)
    while np_ % tn:
        tn //= 2
    out = matmul_pallas(xb, wb, tn=tn, **kw)
    return out[:, :n].reshape(bsz, seq, n)


def _rwkv7_time_mix(proj, b_col, w_up, w0, a_up, a0, g_up, k_k, k_a, r_k, gn_w, gn_b):
    lo = b_col + 3 * B_WIDTH
    xw = proj[:, :, lo:lo + B_DECAY_RANK]
    xa = proj[:, :, lo + B_DECAY_RANK:lo + B_DECAY_RANK + B_ICLR_RANK]
    xg = proj[:, :, lo + B_DECAY_RANK + B_ICLR_RANK:lo + B_DECAY_RANK + B_ICLR_RANK + B_GATE_RANK]
    w_log = -jax.nn.softplus(-(w0 + _mm3(jnp.tanh(xw), w_up))) - 0.5
    a = jax.nn.sigmoid(a0 + _mm3(xa, a_up))
    g = _mm3(jax.nn.sigmoid(xg), g_up)
    return wkv7_pallas(proj, b_col, a, -jnp.exp(w_log), g, k_k, k_a, r_k, gn_w, gn_b)


CONV_TAIL = 8


def _ssd_kernel(x_ref, bm_ref, cm_ref, z_ref, wx_ref, wb_ref, wc_ref, bx_ref, bb_ref, bc_ref,
                dt_ref, dtt_ref, da_ref, dat_ref, dskip_ref, gn_ref, o_ref,
                state_ref, tx_ref, tb_ref, tc_ref, *, tb, hpg, hdim):
    step = pl.program_id(2)
    width = hpg * hdim

    @pl.when(step == 0)
    def _():
        state_ref[...] = jnp.zeros_like(state_ref)
        tx_ref[...] = jnp.zeros_like(tx_ref)
        tb_ref[...] = jnp.zeros_like(tb_ref)
        tc_ref[...] = jnp.zeros_like(tc_ref)

    def conv_silu(cur_ref, tail_ref, w_ref, b_ref):
        cur = cur_ref[...]
        ext = jnp.concatenate([tail_ref[...], cur], axis=0)
        w = w_ref[...]
        acc = b_ref[...]
        for j in range(C_CONV):
            off = CONV_TAIL - (C_CONV - 1) + j
            acc = acc + w[j:j + 1, :] * ext[off:off + tb, :]
        tail_ref[...] = cur[tb - CONV_TAIL:, :]
        return acc * jax.nn.sigmoid(acc)

    xs = conv_silu(x_ref, tx_ref, wx_ref, bx_ref)
    bm = conv_silu(bm_ref, tb_ref, wb_ref, bb_ref)
    cm = conv_silu(cm_ref, tc_ref, wc_ref, bc_ref)

    row = lax.broadcasted_iota(I32, (CHUNK, CHUNK), 0)
    col = lax.broadcasted_iota(I32, (CHUNK, CHUNK), 1)
    causal = row >= col
    tri = causal.astype(F32)
    triu = (row <= col).astype(F32)
    eh = (lax.broadcasted_iota(I32, (hpg, width), 1) // hdim == lax.broadcasted_iota(I32, (hpg, width), 0)).astype(F32)
    lane_in_pair = lax.broadcasted_iota(I32, (CHUNK, 2 * hdim), 1)

    ys = []
    for c in range(tb // CHUNK):
        lo = c * CHUNK
        xc = xs[lo:lo + CHUNK]
        bc = bm[lo:lo + CHUNK]
        cc = cm[lo:lo + CHUNK]
        dt = dt_ref[0, 0, lo:lo + CHUNK, :]
        dtt = dtt_ref[0, 0, :, lo:lo + CHUNK]
        cum = _sel_dot(tri, da_ref[0, 0, lo:lo + CHUNK, :])
        cumt = _dot_sel(dat_ref[0, 0, :, lo:lo + CHUNK], triu)
        cb = _bdot_nt(cc, bc)
        pairs = []
        for hp in range(hpg // 2):
            xp = xc[:, hp * 2 * hdim:(hp + 1) * 2 * hdim]
            outs = []
            for h in (2 * hp, 2 * hp + 1):
                seg = cum[:, h:h + 1] - cumt[h:h + 1, :]
                wts = cb * jnp.exp(jnp.where(causal, seg, -jnp.inf)) * dtt[h:h + 1, :]
                outs.append(_bdot(wts, xp))
            pairs.append(jnp.where(lane_in_pair < hdim, outs[0], outs[1]))
        y = jnp.concatenate(pairs, axis=1)
        ecum = jnp.exp(cum)
        y = y + _bdot(cc, state_ref[...]) * _dot_sel(ecum, eh)
        last = cum[CHUNK - 1:CHUNK, :]
        to_end = jnp.exp(last - cum) * dt
        xw = xc * _dot_sel(to_end, eh)
        sdec = _dot_sel(jnp.exp(last), eh)
        state_ref[...] = state_ref[...] * sdec + _bdot_tn(bc, xw)
        ys.append(y)
    y = jnp.concatenate(ys, axis=0)
    y = y + dskip_ref[...] * xs
    z = z_ref[...].astype(F32)
    y = y * (z * jax.nn.sigmoid(z))
    y = y * lax.rsqrt(jnp.mean(y * y, axis=-1, keepdims=True) + NORM_EPS) * gn_ref[...]
    o_ref[...] = y.astype(o_ref.dtype)


def ssd_mixer_pallas(z, xbc, dt_raw, conv_w, conv_b, dt_bias, a_log, d_skip, gate_norm, bsz, seq, tb=512):
    m, c_inner = z.shape
    nh = dt_raw.shape[1]
    hpg = nh // C_GROUPS
    width = c_inner // C_GROUPS
    assert width == hpg * C_HEAD_DIM and width % LANES == 0 and C_STATE % LANES == 0 and hpg % 2 == 0
    steps = seq // tb
    dt = jax.nn.softplus(dt_raw.astype(F32) + dt_bias.astype(F32)).reshape(bsz, seq, C_GROUPS, hpg)
    da = dt * -jnp.exp(a_log.astype(F32)).reshape(C_GROUPS, hpg)
    dt_g = jnp.transpose(dt, (0, 2, 1, 3))
    da_g = jnp.transpose(da, (0, 2, 1, 3))
    dt_t = jnp.transpose(dt, (0, 2, 3, 1))
    da_t = jnp.transpose(da, (0, 2, 3, 1))
    dskip = jnp.repeat(d_skip.astype(F32), C_HEAD_DIM).reshape(C_GROUPS, 1, width)
    gn = gate_norm.astype(F32).reshape(C_GROUPS, 1, width)
    cw = conv_w.astype(F32)
    cbias = conv_b.astype(F32).reshape(1, -1)
    boff = c_inner // C_STATE
    coff = boff + C_GROUPS

    row_ix = lambda b, s: b * steps + s
    small = pl.BlockSpec((1, 1, tb, hpg), lambda b, g, s: (b, g, s, 0))
    small_t = pl.BlockSpec((1, 1, hpg, tb), lambda b, g, s: (b, g, 0, s))
    per_group = pl.BlockSpec((None, 1, width), lambda b, g, s: (g, 0, 0))
    in_specs = [
        pl.BlockSpec((tb, width), lambda b, g, s: (row_ix(b, s), g)),
        pl.BlockSpec((tb, C_STATE), lambda b, g, s: (row_ix(b, s), boff + g)),
        pl.BlockSpec((tb, C_STATE), lambda b, g, s: (row_ix(b, s), coff + g)),
        pl.BlockSpec((tb, width), lambda b, g, s: (row_ix(b, s), g)),
        pl.BlockSpec((C_CONV, width), lambda b, g, s: (0, g)),
        pl.BlockSpec((C_CONV, C_STATE), lambda b, g, s: (0, boff + g)),
        pl.BlockSpec((C_CONV, C_STATE), lambda b, g, s: (0, coff + g)),
        pl.BlockSpec((1, width), lambda b, g, s: (0, g)),
        pl.BlockSpec((1, C_STATE), lambda b, g, s: (0, boff + g)),
        pl.BlockSpec((1, C_STATE), lambda b, g, s: (0, coff + g)),
        small, small_t, small, small_t, per_group, per_group,
    ]
    return pl.pallas_call(
        functools.partial(_ssd_kernel, tb=tb, hpg=hpg, hdim=C_HEAD_DIM),
        grid=(bsz, C_GROUPS, steps),
        in_specs=in_specs,
        out_specs=pl.BlockSpec((tb, width), lambda b, g, s: (row_ix(b, s), g)),
        out_shape=jax.ShapeDtypeStruct((m, c_inner), BF16),
        scratch_shapes=[pltpu.VMEM((C_STATE, width), F32), pltpu.VMEM((CONV_TAIL, width), F32),
                        pltpu.VMEM((CONV_TAIL, C_STATE), F32), pltpu.VMEM((CONV_TAIL, C_STATE), F32)],
        compiler_params=pltpu.CompilerParams(
            dimension_semantics=("parallel", "parallel", "arbitrary"), vmem_limit_bytes=VMEM_LIMIT_BYTES),
        name="ssd",
    )(xbc, xbc, xbc, z, cw, cw, cw, cbias, cbias, cbias, dt_g, dt_t, da_g, da_t, dskip, gn)


def _dsa_rwkv_layer(x, positions, norm_mix, w_in, q_norm, w_uq, w_qi, kv_norm, w_uk, w_uv,
                    shift_mu, w_up, w0, a_up, a0, g_up, k_k, k_a, r_k, gn_w, gn_b, w_out,
                    norm_ffn, ffn_gate, ffn_up, ffn_down):
    bsz, seq, d = x.shape
    m = bsz * seq
    x2 = x.reshape(m, d)
    cos_a, sin_a = _rope_tables(positions, A_HEAD_DIM)
    cos_i, sin_i = _rope_tables(positions, IDX_DIM)

    b_in = w_in.shape[1] - A_IN
    b_col = _round_up(A_IN, 8 * B_HEAD_DIM)
    width = _round_up(b_col + b_in, 256)
    w_in_b = w_in.astype(BF16)
    w_in_p = jnp.concatenate([_pad_axis(w_in_b[:, :A_IN], 1, b_col), _pad_axis(w_in_b[:, A_IN:], 1, width - b_col)], axis=1)
    mu_p = jnp.concatenate([jnp.zeros((b_col,), F32), _pad_axis(shift_mu.astype(F32), 0, width - b_col)])
    proj = matmul_shift_pallas(rmsnorm_pallas(x2, norm_mix), w_in_p, mu_p, seq, name="l0_in").reshape(bsz, seq, width)
    c_q = proj[:, :, :A_Q_RANK]
    c_kv = proj[:, :, A_Q_RANK:A_Q_RANK + A_KV_RANK]
    k_idx = proj[:, :, A_Q_RANK + A_KV_RANK:A_Q_RANK + A_KV_RANK + IDX_DIM]
    w_idx = proj[:, :, A_Q_RANK + A_KV_RANK + IDX_DIM:A_IN]
    c_q = _rmsnorm_jax(c_q, q_norm)
    c_kv = _rmsnorm_jax(c_kv, kv_norm)
    c_q2 = c_q.reshape(m, A_Q_RANK).astype(BF16)
    rope_cs = lambda cos, sin: (jnp.concatenate([cos, cos], -1).reshape(m, -1), jnp.concatenate([-sin, sin], -1).reshape(m, -1))
    q = matmul_rope_pallas(c_q2, w_uq.astype(BF16), *rope_cs(cos_a, sin_a), scale=A_HEAD_DIM ** -0.5, name="l0_q")
    q_idx = matmul_rope_pallas(c_q2, w_qi.astype(BF16), *rope_cs(cos_i, sin_i), name="l0_qidx")
    kv = _mm3(c_kv, jnp.concatenate([w_uk, w_uv], axis=1))
    k = _apply_rope(kv[:, :, None, :A_HEAD_DIM], cos_a, sin_a)[:, :, 0]
    v = kv[:, :, A_HEAD_DIM:]
    k_idx = _apply_rope(k_idx[:, :, None, :], cos_i, sin_i)[:, :, 0]
    w_idx = w_idx * (IDX_HEADS * IDX_DIM) ** -0.5
    y_a = dsa_pallas(q.reshape(bsz, seq, -1), q_idx.reshape(bsz, seq, -1), w_idx, k, v, k_idx,
                     topk=min(IDX_TOPK_MAX, seq // 4), ha=A_HEADS, hi=IDX_HEADS)
    y_b = _rwkv7_time_mix(proj, b_col, w_up, w0, a_up, a0, g_up, k_k, k_a, r_k, gn_w, gn_b)
    y = jnp.concatenate([y_a.astype(BF16), y_b], axis=-1).reshape(m, -1)
    x2 = matmul_pallas(y, w_out.astype(BF16), res=x2, name="l0_out")

    ffn = ffn_gate.shape[1]
    ffn_p = _round_up(ffn, 512)
    h = rmsnorm_pallas(x2, norm_ffn)
    wg = _pad_axis(ffn_gate.astype(BF16), 1, ffn_p)[None]
    wu = _pad_axis(ffn_up.astype(BF16), 1, ffn_p)[None]
    wd = _pad_axis(ffn_down.astype(BF16), 0, ffn_p)
    tm = min(1024, m)
    hh = grouped_glu_pallas(h, wg, wu, jnp.zeros((m // tm,), I32), jnp.full((1,), m // tm, I32), tm=tm, name="l0_glu")
    x2 = matmul_pallas(hh, wd, res=x2, tm=512, tn=256, name="l0_down")
    return x2.reshape(bsz, seq, d)


def _ssd_moe_layer(x, norm_mix, w_in, conv_w, conv_b, dt_bias, a_log, d_skip, gate_norm, w_out,
                   norm_ffn, router, exp_gate, exp_up, exp_down):
    bsz, seq, d = x.shape
    m = bsz * seq
    x2 = x.reshape(m, d)
    c_inner = w_out.shape[0]
    c_conv_dim = c_inner + 2 * C_GROUPS * C_STATE
    h = rmsnorm_pallas(x2, norm_mix)
    w_in_b = w_in.astype(BF16)
    z = matmul_pallas(h, w_in_b[:, :c_inner], name="l1_in_z")
    xbc = matmul_pallas(h, w_in_b[:, c_inner:c_inner + c_conv_dim], name="l1_in_xbc")
    dt_raw = matmul_pallas(h, w_in_b[:, c_inner + c_conv_dim:], name="l1_in_dt")
    y = ssd_mixer_pallas(z, xbc, dt_raw, conv_w, conv_b, dt_bias, a_log, d_skip, gate_norm, bsz, seq)
    x2 = matmul_pallas(y, w_out.astype(BF16), res=x2, tm=512, tn=512, name="l1_out")
    return (x2,) + moe_sparse(x2, norm_ffn, router, exp_gate, exp_up, exp_down)


def kernel(x, positions, l0_norm_mix, l0_w_in, l0_q_norm, l0_w_uq, l0_w_qi, l0_kv_norm, l0_w_uk, l0_w_uv,
           l0_shift_mu, l0_w_up, l0_w0, l0_a_up, l0_a0, l0_g_up, l0_k_k, l0_k_a, l0_r_k, l0_gn_w, l0_gn_b,
           l0_w_out, l0_norm_ffn, l0_ffn_gate, l0_ffn_up, l0_ffn_down,
           l1_norm_mix, l1_w_in, l1_conv_w, l1_conv_b, l1_dt_bias, l1_a_log, l1_d_skip, l1_gate_norm, l1_w_out,
           l1_norm_ffn, l1_router, l1_exp_gate, l1_exp_up, l1_exp_down, final_norm):
    x = _dsa_rwkv_layer(x, positions, l0_norm_mix, l0_w_in, l0_q_norm, l0_w_uq, l0_w_qi, l0_kv_norm, l0_w_uk,
                        l0_w_uv, l0_shift_mu, l0_w_up, l0_w0, l0_a_up, l0_a0, l0_g_up, l0_k_k, l0_k_a, l0_r_k,
                        l0_gn_w, l0_gn_b, l0_w_out, l0_norm_ffn, l0_ffn_gate, l0_ffn_up, l0_ffn_down)
    bsz, seq, d = x.shape
    x2, y1, y2, route = _ssd_moe_layer(x, l1_norm_mix, l1_w_in, l1_conv_w, l1_conv_b, l1_dt_bias, l1_a_log, l1_d_skip,
                                       l1_gate_norm, l1_w_out, l1_norm_ffn, l1_router, l1_exp_gate, l1_exp_up,
                                       l1_exp_down)
    return combine_norm_pallas(x2, y1, y2, route, final_norm).reshape(bsz, seq, d)
```

```python
import functools

import numpy as np
import jax
import jax.numpy as jnp
from jax import lax
from jax.experimental import pallas as pl
from jax.experimental.pallas import tpu as pltpu

F32 = jnp.float32
BF16 = jnp.bfloat16
I32 = jnp.int32
HI = lax.Precision.HIGHEST

CHUNK = 64
QBLOCK = 128
ROPE_THETA = 10000.0
NORM_EPS = 1e-6

A_HEADS = 16
A_HEAD_DIM = 128
A_Q_RANK = 1024
A_KV_RANK = 512
IDX_HEADS = 32
IDX_DIM = 128
IDX_TOPK_MAX = 256

B_HEADS = 32
B_HEAD_DIM = 64
B_WIDTH = B_HEADS * B_HEAD_DIM
B_DECAY_RANK = 128
B_ICLR_RANK = 128
B_GATE_RANK = 480
B_GN_EPS = 64e-5
A_IN = A_Q_RANK + A_KV_RANK + IDX_DIM + IDX_HEADS

C_HEAD_DIM = 64
C_GROUPS = 8
C_STATE = 128
C_CONV = 4

TOP_K_EXPERTS = 2

LANES = 128
VMEM_LIMIT_BYTES = 56 * 1024 * 1024


def _round_up(n, m):
    return (n + m - 1) // m * m


def _pad_axis(a, axis, size):
    if a.shape[axis] == size:
        return a
    pads = [(0, 0)] * a.ndim
    pads[axis] = (0, size - a.shape[axis])
    return jnp.pad(a, pads)


def _rmsnorm_kernel(x_ref, g_ref, o_ref):
    x = x_ref[...].astype(F32)
    ms = jnp.mean(x * x, axis=-1, keepdims=True)
    o_ref[...] = (x * lax.rsqrt(ms + NORM_EPS) * g_ref[...]).astype(o_ref.dtype)


def rmsnorm_pallas(x, g, out_dtype=BF16, tm=256):
    m, d = x.shape
    tm = min(tm, m)
    return pl.pallas_call(
        _rmsnorm_kernel,
        grid=(m // tm,),
        in_specs=[pl.BlockSpec((tm, d), lambda i: (i, 0)),
                  pl.BlockSpec((1, d), lambda i: (0, 0))],
        out_specs=pl.BlockSpec((tm, d), lambda i: (i, 0)),
        out_shape=jax.ShapeDtypeStruct((m, d), out_dtype),
        compiler_params=pltpu.CompilerParams(dimension_semantics=("parallel",)),
        name="rmsnorm",
    )(x, g.reshape(1, d).astype(F32))


def _mm_kernel(x_ref, w_ref, *rest, nk, has_res):
    if has_res:
        res_ref, o_ref, acc_ref = rest
    else:
        o_ref, acc_ref = rest
        res_ref = None
    p = jnp.dot(x_ref[...], w_ref[...], preferred_element_type=F32)

    def finish(acc):
        if has_res:
            acc = acc + res_ref[...].astype(F32)
        o_ref[...] = acc.astype(o_ref.dtype)

    if nk == 1:
        finish(p)
        return
    k = pl.program_id(2)

    @pl.when(k == 0)
    def _():
        acc_ref[...] = p

    @pl.when(jnp.logical_and(k > 0, k < nk - 1))
    def _():
        acc_ref[...] += p

    @pl.when(k == nk - 1)
    def _():
        finish(acc_ref[...] + p)


def matmul_pallas(x, w, res=None, out_dtype=F32, tm=1024, tn=512, tk=None, name="matmul"):
    m, kdim = x.shape
    k2, n = w.shape
    assert kdim == k2
    tm = min(tm, m)
    tn = min(tn, n)
    tk = kdim if tk is None else min(tk, kdim)
    assert m % tm == 0 and n % tn == 0 and kdim % tk == 0, (x.shape, w.shape, tm, tn, tk)
    nk = kdim // tk
    in_specs = [pl.BlockSpec((tm, tk), lambda i, j, k: (i, k)),
                pl.BlockSpec((tk, tn), lambda i, j, k: (k, j))]
    args = [x, w]
    if res is not None:
        in_specs.append(pl.BlockSpec((tm, tn), lambda i, j, k: (i, j)))
        args.append(res)
    acc_shape = (tm, tn) if nk > 1 else (8, 128)
    return pl.pallas_call(
        functools.partial(_mm_kernel, nk=nk, has_res=res is not None),
        grid=(m // tm, n // tn, nk),
        in_specs=in_specs,
        out_specs=pl.BlockSpec((tm, tn), lambda i, j, k: (i, j)),
        out_shape=jax.ShapeDtypeStruct((m, n), out_dtype),
        scratch_shapes=[pltpu.VMEM(acc_shape, F32)],
        compiler_params=pltpu.CompilerParams(
            dimension_semantics=("parallel", "parallel", "arbitrary"),
            vmem_limit_bytes=VMEM_LIMIT_BYTES),
        name=name,
    )(*args)


def _mm_rope_kernel(x_ref, w_ref, cos_ref, sin_ref, o_ref, *, scale, head_dim):
    p = jnp.dot(x_ref[...], w_ref[...], preferred_element_type=F32) * scale
    cos = cos_ref[...]
    sin = sin_ref[...]
    heads = []
    for h in range(p.shape[1] // head_dim):
        ph = p[:, h * head_dim:(h + 1) * head_dim]
        heads.append(ph * cos + pltpu.roll(ph, head_dim // 2, 1) * sin)
    o_ref[...] = jnp.concatenate(heads, axis=1).astype(o_ref.dtype)


def matmul_rope_pallas(x, w, cos_full, sin_signed, scale=1.0, tm=1024, tn=512, name="matmul_rope"):
    m, kdim = x.shape
    n = w.shape[1]
    hd = cos_full.shape[1]
    tm = min(tm, m)
    assert m % tm == 0 and n % tn == 0 and tn % hd == 0
    return pl.pallas_call(
        functools.partial(_mm_rope_kernel, scale=scale, head_dim=hd),
        grid=(m // tm, n // tn),
        in_specs=[pl.BlockSpec((tm, kdim), lambda i, j: (i, 0)),
                  pl.BlockSpec((kdim, tn), lambda i, j: (0, j)),
                  pl.BlockSpec((tm, hd), lambda i, j: (i, 0)),
                  pl.BlockSpec((tm, hd), lambda i, j: (i, 0))],
        out_specs=pl.BlockSpec((tm, tn), lambda i, j: (i, j)),
        out_shape=jax.ShapeDtypeStruct((m, n), BF16),
        compiler_params=pltpu.CompilerParams(
            dimension_semantics=("parallel", "parallel"), vmem_limit_bytes=VMEM_LIMIT_BYTES),
        name=name,
    )(x, w, cos_full, sin_signed)


SHIFT_TAIL = 8


def _mm_shift_kernel(x_ref, w_ref, mu_ref, o_ref, carry_ref, *, tiles_per_seq):
    i = pl.program_id(0)
    j = pl.program_id(1)

    @pl.when(i == 0)
    def _():
        carry_ref[j] = jnp.zeros(carry_ref.shape[1:], F32)

    p = jnp.dot(x_ref[...], w_ref[...], preferred_element_type=F32)
    tm = p.shape[0]
    tail = jnp.where(i % tiles_per_seq == 0, 0.0, carry_ref[j])
    prev = jnp.concatenate([tail, p], axis=0)[SHIFT_TAIL - 1:SHIFT_TAIL - 1 + tm]
    carry_ref[j] = p[tm - SHIFT_TAIL:]
    o_ref[...] = p + (prev - p) * mu_ref[...]


def matmul_shift_pallas(x, w, mu, seq, tm=1024, tn=256, name="matmul_shift"):
    m, kdim = x.shape
    n = w.shape[1]
    tm = min(tm, seq)
    assert seq % tm == 0 and m % seq == 0 and n % tn == 0
    return pl.pallas_call(
        functools.partial(_mm_shift_kernel, tiles_per_seq=seq // tm),
        grid=(m // tm, n // tn),
        in_specs=[pl.BlockSpec((tm, kdim), lambda i, j: (i, 0)),
                  pl.BlockSpec((kdim, tn), lambda i, j: (0, j)),
                  pl.BlockSpec((1, tn), lambda i, j: (0, j))],
        out_specs=pl.BlockSpec((tm, tn), lambda i, j: (i, j)),
        out_shape=jax.ShapeDtypeStruct((m, n), F32),
        scratch_shapes=[pltpu.VMEM((n // tn, SHIFT_TAIL, tn), F32)],
        compiler_params=pltpu.CompilerParams(
            dimension_semantics=("arbitrary", "arbitrary"), vmem_limit_bytes=VMEM_LIMIT_BYTES),
        name=name,
    )(x, w, mu.reshape(1, n).astype(F32))


def _gglu_kernel(te_ref, nu_ref, x_ref, wg_ref, wu_ref, o_ref, *, row_axis):
    t = pl.program_id(row_axis)

    @pl.when(t < nu_ref[0])
    def _():
        x = x_ref[...]
        g = jnp.dot(x, wg_ref[...].astype(BF16), preferred_element_type=F32)
        u = jnp.dot(x, wu_ref[...].astype(BF16), preferred_element_type=F32)
        o_ref[...] = (g * jax.nn.sigmoid(g) * u).astype(o_ref.dtype)

    @pl.when(t >= nu_ref[0])
    def _():
        o_ref[...] = jnp.zeros_like(o_ref)


def _grouped_grid(r, n, tm, tn, weights_outer):
    if weights_outer:
        return (n // tn, r // tm), 1, lambda f: (lambda j, t, te, nu: f(t, j, te))
    return (r // tm, n // tn), 0, lambda f: (lambda t, j, te, nu: f(t, j, te))


def grouped_glu_pallas(xs, wg, wu, tile_expert, n_used, tm, tn=512, weights_outer=False, name="glu"):
    r, kdim = xs.shape
    n = wg.shape[2]
    assert r % tm == 0 and n % tn == 0
    grid, row_axis, ix = _grouped_grid(r, n, tm, tn, weights_outer)
    grid_spec = pltpu.PrefetchScalarGridSpec(
        num_scalar_prefetch=2,
        grid=grid,
        in_specs=[pl.BlockSpec((tm, kdim), ix(lambda t, j, te: (t, 0))),
                  pl.BlockSpec((None, kdim, tn), ix(lambda t, j, te: (te[t], 0, j))),
                  pl.BlockSpec((None, kdim, tn), ix(lambda t, j, te: (te[t], 0, j)))],
        out_specs=pl.BlockSpec((tm, tn), ix(lambda t, j, te: (t, j))),
    )
    return pl.pallas_call(
        functools.partial(_gglu_kernel, row_axis=row_axis),
        grid_spec=grid_spec,
        out_shape=jax.ShapeDtypeStruct((r, n), BF16),
        compiler_params=pltpu.CompilerParams(
            dimension_semantics=("arbitrary", "arbitrary"), vmem_limit_bytes=VMEM_LIMIT_BYTES),
        name=name,
    )(tile_expert, n_used, xs, wg, wu)


def _gdown_kernel(te_ref, nu_ref, x_ref, w_ref, o_ref, *, row_axis):
    t = pl.program_id(row_axis)

    @pl.when(t < nu_ref[0])
    def _():
        o_ref[...] = jnp.dot(x_ref[...], w_ref[...].astype(BF16), preferred_element_type=F32).astype(o_ref.dtype)

    @pl.when(t >= nu_ref[0])
    def _():
        o_ref[...] = jnp.zeros_like(o_ref)


def grouped_down_pallas(hs, wd, tile_expert, n_used, tm, tn=512, weights_outer=False, name="down"):
    r, kdim = hs.shape
    n = wd.shape[2]
    assert r % tm == 0 and n % tn == 0
    grid, row_axis, ix = _grouped_grid(r, n, tm, tn, weights_outer)
    grid_spec = pltpu.PrefetchScalarGridSpec(
        num_scalar_prefetch=2,
        grid=grid,
        in_specs=[pl.BlockSpec((tm, kdim), ix(lambda t, j, te: (t, 0))),
                  pl.BlockSpec((None, kdim, tn), ix(lambda t, j, te: (te[t], 0, j)))],
        out_specs=pl.BlockSpec((tm, tn), ix(lambda t, j, te: (t, j))),
    )
    return pl.pallas_call(
        functools.partial(_gdown_kernel, row_axis=row_axis),
        grid_spec=grid_spec,
        out_shape=jax.ShapeDtypeStruct((r, n), BF16),
        compiler_params=pltpu.CompilerParams(
            dimension_semantics=("arbitrary", "arbitrary"), vmem_limit_bytes=VMEM_LIMIT_BYTES),
        name=name,
    )(tile_expert, n_used, hs, wd)


def _router_kernel(x_ref, g_ref, wr_ref, h_ref, r_ref, *, n_experts):
    x = x_ref[...]
    h = x * lax.rsqrt(jnp.mean(x * x, axis=-1, keepdims=True) + NORM_EPS) * g_ref[...]
    h_ref[...] = h.astype(h_ref.dtype)
    logits = jnp.dot(h, wr_ref[...], preferred_element_type=F32, precision=HI)
    lane = lax.broadcasted_iota(I32, logits.shape, 1)
    logits = jnp.where(lane < n_experts, logits, -jnp.inf)
    m1 = jnp.max(logits, axis=1, keepdims=True)
    i1 = jnp.min(jnp.where(logits == m1, lane, LANES), axis=1, keepdims=True)
    rest = jnp.where(lane == i1, -jnp.inf, logits)
    m2 = jnp.max(rest, axis=1, keepdims=True)
    i2 = jnp.min(jnp.where(rest == m2, lane, LANES), axis=1, keepdims=True)
    e2 = jnp.exp(m2 - m1)
    w1 = 1.0 / (1.0 + e2)
    w2 = e2 / (1.0 + e2)
    out = jnp.where(lane == 0, i1.astype(F32), 0.0)
    out = jnp.where(lane == 1, i2.astype(F32), out)
    out = jnp.where(lane == 2, w1, out)
    out = jnp.where(lane == 3, w2, out)
    r_ref[...] = out


def router_pallas(x, g, router, tm=256):
    m, d = x.shape
    ne = router.shape[1]
    wr = jnp.pad(router.astype(F32), ((0, 0), (0, LANES - ne)))
    return pl.pallas_call(
        functools.partial(_router_kernel, n_experts=ne),
        grid=(m // tm,),
        in_specs=[pl.BlockSpec((tm, d), lambda i: (i, 0)),
                  pl.BlockSpec((1, d), lambda i: (0, 0)),
                  pl.BlockSpec((d, LANES), lambda i: (0, 0))],
        out_specs=[pl.BlockSpec((tm, d), lambda i: (i, 0)),
                   pl.BlockSpec((tm, LANES), lambda i: (i, 0))],
        out_shape=[jax.ShapeDtypeStruct((m, d), BF16), jax.ShapeDtypeStruct((m, LANES), F32)],
        compiler_params=pltpu.CompilerParams(dimension_semantics=("parallel",)),
        name="router",
    )(x, g.reshape(1, d).astype(F32), wr)


def moe_sparse(x2, norm_ffn, router, exp_gate, exp_up, exp_down, tm=512):
    m, d = x2.shape
    ne = router.shape[1]
    h, route = router_pallas(x2, norm_ffn, router)
    idx = route[:, :TOP_K_EXPERTS].astype(I32)
    flat_e = idx.reshape(-1)
    onehot = (flat_e[:, None] == jnp.arange(ne, dtype=I32)[None, :]).astype(I32)
    rank = jnp.sum((jnp.cumsum(onehot, axis=0) - 1) * onehot, axis=1)
    count = jnp.sum(onehot, axis=0)
    padded = (count + tm - 1) // tm * tm
    ends = jnp.cumsum(padded)
    pos = (ends - padded)[flat_e] + rank
    rows = m * TOP_K_EXPERTS + ne * tm
    n_tiles = rows // tm
    token_of_row = jnp.zeros((rows,), I32).at[pos].set(jnp.arange(m * TOP_K_EXPERTS, dtype=I32) // TOP_K_EXPERTS)
    tile_start = jnp.arange(n_tiles, dtype=I32)[:, None] * tm
    tile_expert = jnp.minimum(jnp.sum((tile_start >= ends[None, :]).astype(I32), axis=1), ne - 1).astype(I32)
    n_used = (ends[-1] // tm).astype(I32).reshape(1)
    xs = jnp.take(h, token_of_row, axis=0)
    hh = grouped_glu_pallas(xs, exp_gate, exp_up, tile_expert, n_used, tm, weights_outer=True, name="moe_glu")
    ys = grouped_down_pallas(hh, exp_down, tile_expert, n_used, tm, weights_outer=True, name="moe_down")
    pos2 = pos.reshape(m, TOP_K_EXPERTS)
    return jnp.take(ys, pos2[:, 0], axis=0), jnp.take(ys, pos2[:, 1], axis=0), route


def _combine_norm_kernel(x_ref, y1_ref, y2_ref, r_ref, g_ref, o_ref):
    r = r_ref[...]
    w1 = r[:, TOP_K_EXPERTS:TOP_K_EXPERTS + 1]
    w2 = r[:, TOP_K_EXPERTS + 1:TOP_K_EXPERTS + 2]
    x = x_ref[...] + y1_ref[...].astype(F32) * w1 + y2_ref[...].astype(F32) * w2
    ms = jnp.mean(x * x, axis=-1, keepdims=True)
    o_ref[...] = (x * lax.rsqrt(ms + NORM_EPS) * g_ref[...]).astype(o_ref.dtype)


def combine_norm_pallas(x, y1, y2, route, g, tm=256):
    m, d = x.shape
    row = pl.BlockSpec((tm, d), lambda i: (i, 0))
    return pl.pallas_call(
        _combine_norm_kernel,
        grid=(m // tm,),
        in_specs=[row, row, row, pl.BlockSpec((tm, LANES), lambda i: (i, 0)), pl.BlockSpec((1, d), lambda i: (0, 0))],
        out_specs=row,
        out_shape=jax.ShapeDtypeStruct((m, d), x.dtype),
        compiler_params=pltpu.CompilerParams(dimension_semantics=("parallel",)),
        name="combine_norm",
    )(x, y1, y2, route, g.reshape(1, d).astype(F32))


def _rmsnorm_jax(x, g):
    xf = x.astype(F32)
    return xf * lax.rsqrt(jnp.mean(xf * xf, axis=-1, keepdims=True) + NORM_EPS) * g.astype(F32)


def _rope_tables(positions, dim):
    inv_freq = ROPE_THETA ** (-jnp.arange(0, dim, 2, dtype=F32) / dim)
    ang = positions.astype(F32)[..., None] * inv_freq
    return jnp.cos(ang)[:, :, None, :], jnp.sin(ang)[:, :, None, :]


def _apply_rope(x, cos, sin):
    x1, x2 = jnp.split(x.astype(F32), 2, axis=-1)
    return jnp.concatenate([x1 * cos - x2 * sin, x2 * cos + x1 * sin], axis=-1)


NEG_BIG = -1e30
INT_MIN = np.int32(-2 ** 31)


def _dsa_kernel(qi_ref, w_ref, q_ref, kit_ref, kt_ref, v_ref, o_ref,
                s_ref, wb_ref, m_ref, l_ref, acc_ref, qis_ref, qs_ref, *, tk, idx_heads, att_heads, topk, head_group):
    i = pl.program_id(1)
    hd_i = qi_ref.shape[2] // idx_heads
    for h in range(idx_heads):
        qis_ref[h * QBLOCK:(h + 1) * QBLOCK, :] = qi_ref[0, :, h * hd_i:(h + 1) * hd_i]
    hd_a = q_ref.shape[2] // att_heads
    for h in range(att_heads):
        qs_ref[h * QBLOCK:(h + 1) * QBLOCK, :] = q_ref[0, :, h * hd_a:(h + 1) * hd_a]
    n_tiles = ((i + 1) * QBLOCK + tk - 1) // tk
    nl = tk // LANES
    row = lax.broadcasted_iota(I32, (QBLOCK, tk), 0)
    col = lax.broadcasted_iota(I32, (QBLOCK, tk), 1)
    limit = (2 * i + 1 + (row >= CHUNK).astype(I32)) * CHUNK

    def admissible(t):
        return (t * tk + col) < limit

    def fold_lanes(x, op):
        out = x[:, :LANES]
        for j in range(1, nl):
            out = op(out, x[:, j * LANES:(j + 1) * LANES])
        return out

    w = w_ref[0, 0]
    for h in range(idx_heads):
        wb_ref[h] = jnp.broadcast_to(w[:, h:h + 1], (QBLOCK, LANES))

    def idx_body(t, carry):
        kit = kit_ref[0, t]
        acc = jnp.zeros((QBLOCK, tk), F32)
        for g in range(idx_heads // head_group):
            rows = head_group * QBLOCK
            rel = jnp.dot(qis_ref[g * rows:(g + 1) * rows, :], kit, preferred_element_type=F32)
            for hh in range(head_group):
                h = g * head_group + hh
                relu = jnp.maximum(rel[hh * QBLOCK:(hh + 1) * QBLOCK], 0.0)
                acc = acc + jnp.concatenate([wb_ref[h]] * nl, axis=1) * relu
        score = jnp.where(admissible(t), acc, -jnp.inf)
        bits = lax.bitcast_convert_type(score, I32)
        s_ref[t] = bits ^ ((bits >> 31) & np.int32(0x7FFFFFFF))
        return carry

    lax.fori_loop(0, n_tiles, idx_body, 0)

    def bit_body(it, ans):
        cand = ans | lax.shift_left(np.int32(1), 31 - it)
        cand_s = cand ^ INT_MIN

        def cnt_body(t, cnt):
            return cnt + fold_lanes((s_ref[t] >= cand_s).astype(I32), jnp.add)

        cnt = lax.fori_loop(0, n_tiles, cnt_body, jnp.zeros((QBLOCK, LANES), I32))
        total = jnp.sum(cnt, axis=1, keepdims=True)
        return jnp.where(total >= topk, cand, ans)

    ans = lax.fori_loop(0, 32, bit_body, jnp.zeros((QBLOCK, 1), I32))
    thr = ans ^ INT_MIN

    def masked_logits(t):
        sel = jnp.logical_and(s_ref[t] >= thr, admissible(t))
        bias = jnp.where(sel, 0.0, NEG_BIG)
        logits = jnp.dot(qs_ref[...], kt_ref[0, t], preferred_element_type=F32)
        return logits + jnp.concatenate([bias] * att_heads, axis=0)

    m_ref[...] = jnp.full_like(m_ref, NEG_BIG)

    def max_body(t, carry):
        m_ref[...] = jnp.maximum(m_ref[...], fold_lanes(masked_logits(t), jnp.maximum))
        return carry

    lax.fori_loop(0, n_tiles, max_body, 0)
    m_ref[...] = jnp.broadcast_to(jnp.max(m_ref[...], axis=1, keepdims=True), m_ref.shape)

    l_ref[...] = jnp.zeros_like(l_ref)
    acc_ref[...] = jnp.zeros_like(acc_ref)

    def att_body(t, carry):
        p = jnp.exp(masked_logits(t) - jnp.concatenate([m_ref[...]] * nl, axis=1))
        l_ref[...] += fold_lanes(p, jnp.add)
        acc_ref[...] += jnp.dot(p.astype(BF16), v_ref[0, t], preferred_element_type=F32)
        return carry

    lax.fori_loop(0, n_tiles, att_body, 0)
    out = acc_ref[...] / jnp.sum(l_ref[...], axis=1, keepdims=True)
    d = out.shape[1]
    for h in range(att_heads):
        o_ref[0, :, h * d:(h + 1) * d] = out[h * QBLOCK:(h + 1) * QBLOCK, :]


def dsa_pallas(q, q_idx, w_idx, k, v, k_idx, topk, ha, hi, tk=512, head_group=8):
    bsz, seq, _ = q.shape
    d = q.shape[2] // ha
    di = q_idx.shape[2] // hi
    nb = seq // QBLOCK
    nt = seq // tk
    wb = w_idx.astype(F32).reshape(bsz, nb, QBLOCK, hi)
    kit = jnp.transpose(k_idx.astype(BF16).reshape(bsz, nt, tk, di), (0, 1, 3, 2))
    kt = jnp.transpose(k.astype(BF16).reshape(bsz, nt, tk, d), (0, 1, 3, 2))
    vb = v.astype(BF16).reshape(bsz, nt, tk, d)
    return pl.pallas_call(
        functools.partial(_dsa_kernel, tk=tk, idx_heads=hi, att_heads=ha, topk=topk, head_group=head_group),
        grid=(bsz, nb),
        in_specs=[
            pl.BlockSpec((1, QBLOCK, hi * di), lambda b, i: (b, i, 0)),
            pl.BlockSpec((1, 1, QBLOCK, hi), lambda b, i: (b, i, 0, 0)),
            pl.BlockSpec((1, QBLOCK, ha * d), lambda b, i: (b, i, 0)),
            pl.BlockSpec((1, nt, di, tk), lambda b, i: (b, 0, 0, 0)),
            pl.BlockSpec((1, nt, d, tk), lambda b, i: (b, 0, 0, 0)),
            pl.BlockSpec((1, nt, tk, d), lambda b, i: (b, 0, 0, 0)),
        ],
        out_specs=pl.BlockSpec((1, QBLOCK, ha * d), lambda b, i: (b, i, 0)),
        out_shape=jax.ShapeDtypeStruct((bsz, seq, ha * d), F32),
        scratch_shapes=[
            pltpu.VMEM((nt, QBLOCK, tk), I32),
            pltpu.VMEM((hi, QBLOCK, LANES), F32),
            pltpu.VMEM((ha * QBLOCK, LANES), F32),
            pltpu.VMEM((ha * QBLOCK, LANES), F32),
            pltpu.VMEM((ha * QBLOCK, d), F32),
            pltpu.VMEM((hi * QBLOCK, di), BF16),
            pltpu.VMEM((ha * QBLOCK, d), BF16),
        ],
        compiler_params=pltpu.CompilerParams(
            dimension_semantics=("parallel", "arbitrary"),
            vmem_limit_bytes=VMEM_LIMIT_BYTES),
        name="dsa",
    )(q_idx, wb, q, kit, kt, vb)


def _bdot(a, b):
    return jnp.dot(a.astype(BF16), b.astype(BF16), preferred_element_type=F32)


def _bdot_nt(a, b):
    return lax.dot_general(a.astype(BF16), b.astype(BF16), (((1,), (1,)), ((), ())), preferred_element_type=F32)


def _bdot_tn(a, b):
    return lax.dot_general(a.astype(BF16), b.astype(BF16), (((0,), (0,)), ((), ())), preferred_element_type=F32)


def _split3(x):
    hi = x.astype(BF16)
    r1 = x - hi.astype(F32)
    mid = r1.astype(BF16)
    return hi, mid, (r1 - mid.astype(F32)).astype(BF16)


def _sel_dot(sel, x):
    hi, mid, lo = _split3(x)
    s = sel.astype(BF16)
    return (jnp.dot(s, hi, preferred_element_type=F32) + jnp.dot(s, mid, preferred_element_type=F32)
            + jnp.dot(s, lo, preferred_element_type=F32))


def _dot_sel(x, sel):
    hi, mid, lo = _split3(x)
    s = sel.astype(BF16)
    return (jnp.dot(hi, s, preferred_element_type=F32) + jnp.dot(mid, s, preferred_element_type=F32)
            + jnp.dot(lo, s, preferred_element_type=F32))


def _wkv7_kernel(r_ref, k_ref, v_ref, a_ref, lw_ref, g_ref, kk_ref, ka_ref, rk_ref, gnw_ref, gnb_ref,
                 y_ref, z_ref, p_ref, *, heads, clen):
    c = pl.program_id(2)
    row = lax.broadcasted_iota(I32, (clen, clen), 0)
    col = lax.broadcasted_iota(I32, (clen, clen), 1)
    strict = row > col
    incl = row >= col
    tri = incl.astype(F32)
    eye = (row == col).astype(F32)

    @pl.when(c == 0)
    def _():
        z_ref[...] = jnp.zeros_like(z_ref)
        p_ref[...] = jnp.ones_like(p_ref)

    hs = range(heads)
    cols = [slice(h * clen, (h + 1) * clen) for h in hs]
    r = [r_ref[0, :, cols[h]] for h in hs]
    k0 = [k_ref[0, :, cols[h]] for h in hs]
    v = [v_ref[0, :, cols[h]] for h in hs]
    ai = [a_ref[0, :, cols[h]] for h in hs]
    lw = [lw_ref[0, :, cols[h]] for h in hs]
    kk = [k0[h] * kk_ref[:, cols[h]] for h in hs]
    kk = [kk[h] * lax.rsqrt(jnp.maximum(jnp.sum(kk[h] * kk[h], axis=-1, keepdims=True), 1e-24)) for h in hs]
    k = [k0[h] * (1.0 + (ai[h] - 1.0) * ka_ref[:, cols[h]]) for h in hs]
    cum = [_sel_dot(tri, lw[h]) for h in hs]
    ecum = [jnp.exp(cum[h]) for h in hs]
    encum = [jnp.exp(-cum[h]) for h in hs]
    rt = [r[h] * ecum[h] for h in hs]
    kt = [k[h] * encum[h] for h in hs]
    bt = [kk[h] * ai[h] * encum[h] for h in hs]
    at = [-kk[h] * jnp.exp(cum[h] - lw[h]) for h in hs]
    x = [jnp.concatenate([at[h], rt[h]], axis=0) for h in hs]
    aab = [_bdot_nt(x[h], bt[h]) for h in hs]
    aak = [_bdot_nt(x[h], kt[h]) for h in hs]
    a_ab = [jnp.where(strict, aab[h][:clen], 0.0) for h in hs]
    a_ak = [jnp.where(strict, aak[h][:clen], 0.0) for h in hs]
    a_rb = [jnp.where(incl, aab[h][clen:], 0.0) for h in hs]
    a_rk = [jnp.where(incl, aak[h][clen:], 0.0) for h in hs]
    xp = a_ab
    t = [eye + a_ab[h] for h in hs]
    n = 2
    while n < clen:
        xp = [_bdot(xp[h], xp[h]) for h in hs]
        t = [t[h] + _bdot(t[h], xp[h]) for h in hs]
        n *= 2
    av = [_bdot(a_ak[h], v[h]) for h in hs]
    uv = [_bdot(t[h], av[h]) for h in hs]
    at2 = [_bdot(t[h], at[h]) for h in hs]
    rp = [rt[h] + _bdot(a_rb[h], at2[h]) for h in hs]
    yv = [_bdot(a_rb[h], uv[h]) + _bdot(a_rk[h], v[h]) for h in hs]
    mp = [_bdot_tn(bt[h], at2[h]) for h in hs]
    gp = [_bdot_tn(jnp.concatenate([bt[h], kt[h]], axis=0), jnp.concatenate([uv[h], v[h]], axis=0)) for h in hs]
    lhs = [jnp.concatenate([rp[h], eye + mp[h]], axis=0) * p_ref[h] for h in hs]
    out = [_bdot(lhs[h], z_ref[h]) for h in hs]
    for h in hs:
        z_ref[h] = out[h][clen:] + gp[h]
        p_ref[h] = ecum[h][clen - 1:clen, :]
    y = [out[h][:clen] + yv[h] for h in hs]
    mean = [jnp.mean(y[h], axis=-1, keepdims=True) for h in hs]
    var = [jnp.mean(jnp.square(y[h] - mean[h]), axis=-1, keepdims=True) for h in hs]
    yn = [(y[h] - mean[h]) * lax.rsqrt(var[h] + B_GN_EPS) * gnw_ref[:, cols[h]] + gnb_ref[:, cols[h]] for h in hs]
    bonus = [jnp.sum(r[h] * k[h] * rk_ref[:, cols[h]], axis=-1, keepdims=True) * v[h] for h in hs]
    y_ref[0] = (jnp.concatenate([yn[h] + bonus[h] for h in hs], axis=1) * g_ref[0]).astype(y_ref.dtype)


def wkv7_pallas(proj, rkv_col, a, lw, g, k_k, k_a, r_k, gn_w, gn_b, heads_per_step=32, clen=CHUNK):
    bsz, seq, width = a.shape
    nh = width // clen
    gh = min(heads_per_step, nh)
    bw = gh * clen
    assert width % bw == 0 and rkv_col % bw == 0
    nblk = width // bw

    def col_spec(first_block):
        return pl.BlockSpec((1, clen, bw), lambda bi, hi, ci: (bi, ci, first_block + hi))

    par_spec = pl.BlockSpec((1, bw), lambda bi, hi, ci: (0, hi))
    par = lambda t: t.astype(F32).reshape(1, width)
    r0 = rkv_col // bw
    return pl.pallas_call(
        functools.partial(_wkv7_kernel, heads=gh, clen=clen),
        grid=(bsz, nblk, seq // clen),
        in_specs=[col_spec(r0), col_spec(r0 + nblk), col_spec(r0 + 2 * nblk),
                  col_spec(0), col_spec(0), col_spec(0)] + [par_spec] * 5,
        out_specs=col_spec(0),
        out_shape=jax.ShapeDtypeStruct((bsz, seq, width), BF16),
        scratch_shapes=[pltpu.VMEM((gh, clen, clen), F32), pltpu.VMEM((gh, 1, clen), F32)],
        compiler_params=pltpu.CompilerParams(dimension_semantics=("parallel", "parallel", "arbitrary")),
        name="wkv7",
    )(proj, proj, proj, a, lw, g, par(k_k), par(k_a), par(r_k), par(gn_w), par(gn_b))


def _mm3(x, w, **kw):
    bsz, seq, kdim = x.shape
    kp = _round_up(kdim, 128)
    n = w.shape[1]
    np_ = _round_up(n, 128)
    xb = _pad_axis(x.reshape(bsz * seq, kdim).astype(BF16), 1, kp)
    wb = _pad_axis(_pad_axis(w.astype(BF16), 0, kp), 1, np_)
    tn = kw.pop("tn", 512)
    while np_ % tn:
        tn //= 2
    out = matmul_pallas(xb, wb, tn=tn, **kw)
    return out[:, :n].reshape(bsz, seq, n)


def _rwkv7_time_mix(proj, b_col, w_up, w0, a_up, a0, g_up, k_k, k_a, r_k, gn_w, gn_b):
    lo = b_col + 3 * B_WIDTH
    xw = proj[:, :, lo:lo + B_DECAY_RANK]
    xa = proj[:, :, lo + B_DECAY_RANK:lo + B_DECAY_RANK + B_ICLR_RANK]
    xg = proj[:, :, lo + B_DECAY_RANK + B_ICLR_RANK:lo + B_DECAY_RANK + B_ICLR_RANK + B_GATE_RANK]
    w_log = -jax.nn.softplus(-(w0 + _mm3(jnp.tanh(xw), w_up))) - 0.5
    a = jax.nn.sigmoid(a0 + _mm3(xa, a_up))
    g = _mm3(jax.nn.sigmoid(xg), g_up)
    return wkv7_pallas(proj, b_col, a, -jnp.exp(w_log), g, k_k, k_a, r_k, gn_w, gn_b)


CONV_TAIL = 8


def _ssd_kernel(x_ref, bm_ref, cm_ref, z_ref, wx_ref, wb_ref, wc_ref, bx_ref, bb_ref, bc_ref,
                dt_ref, dtt_ref, da_ref, dat_ref, dskip_ref, gn_ref, o_ref,
                state_ref, tx_ref, tb_ref, tc_ref, *, tb, hpg, hdim):
    step = pl.program_id(2)
    width = hpg * hdim

    @pl.when(step == 0)
    def _():
        state_ref[...] = jnp.zeros_like(state_ref)
        tx_ref[...] = jnp.zeros_like(tx_ref)
        tb_ref[...] = jnp.zeros_like(tb_ref)
        tc_ref[...] = jnp.zeros_like(tc_ref)

    def conv_silu(cur_ref, tail_ref, w_ref, b_ref):
        cur = cur_ref[...]
        ext = jnp.concatenate([tail_ref[...], cur], axis=0)
        w = w_ref[...]
        acc = b_ref[...]
        for j in range(C_CONV):
            off = CONV_TAIL - (C_CONV - 1) + j
            acc = acc + w[j:j + 1, :] * ext[off:off + tb, :]
        tail_ref[...] = cur[tb - CONV_TAIL:, :]
        return acc * jax.nn.sigmoid(acc)

    xs = conv_silu(x_ref, tx_ref, wx_ref, bx_ref)
    bm = conv_silu(bm_ref, tb_ref, wb_ref, bb_ref)
    cm = conv_silu(cm_ref, tc_ref, wc_ref, bc_ref)

    row = lax.broadcasted_iota(I32, (CHUNK, CHUNK), 0)
    col = lax.broadcasted_iota(I32, (CHUNK, CHUNK), 1)
    causal = row >= col
    tri = causal.astype(F32)
    triu = (row <= col).astype(F32)
    eh = (lax.broadcasted_iota(I32, (hpg, width), 1) // hdim == lax.broadcasted_iota(I32, (hpg, width), 0)).astype(F32)
    lane_in_pair = lax.broadcasted_iota(I32, (CHUNK, 2 * hdim), 1)

    ys = []
    for c in range(tb // CHUNK):
        lo = c * CHUNK
        xc = xs[lo:lo + CHUNK]
        bc = bm[lo:lo + CHUNK]
        cc = cm[lo:lo + CHUNK]
        dt = dt_ref[0, 0, lo:lo + CHUNK, :]
        dtt = dtt_ref[0, 0, :, lo:lo + CHUNK]
        cum = _sel_dot(tri, da_ref[0, 0, lo:lo + CHUNK, :])
        cumt = _dot_sel(dat_ref[0, 0, :, lo:lo + CHUNK], triu)
        cb = _bdot_nt(cc, bc)
        pairs = []
        for hp in range(hpg // 2):
            xp = xc[:, hp * 2 * hdim:(hp + 1) * 2 * hdim]
            outs = []
            for h in (2 * hp, 2 * hp + 1):
                seg = cum[:, h:h + 1] - cumt[h:h + 1, :]
                wts = cb * jnp.exp(jnp.where(causal, seg, -jnp.inf)) * dtt[h:h + 1, :]
                outs.append(_bdot(wts, xp))
            pairs.append(jnp.where(lane_in_pair < hdim, outs[0], outs[1]))
        y = jnp.concatenate(pairs, axis=1)
        ecum = jnp.exp(cum)
        y = y + _bdot(cc, state_ref[...]) * _dot_sel(ecum, eh)
        last = cum[CHUNK - 1:CHUNK, :]
        to_end = jnp.exp(last - cum) * dt
        xw = xc * _dot_sel(to_end, eh)
        sdec = _dot_sel(jnp.exp(last), eh)
        state_ref[...] = state_ref[...] * sdec + _bdot_tn(bc, xw)
        ys.append(y)
    y = jnp.concatenate(ys, axis=0)
    y = y + dskip_ref[...] * xs
    z = z_ref[...].astype(F32)
    y = y * (z * jax.nn.sigmoid(z))
    y = y * lax.rsqrt(jnp.mean(y * y, axis=-1, keepdims=True) + NORM_EPS) * gn_ref[...]
    o_ref[...] = y.astype(o_ref.dtype)


def ssd_mixer_pallas(z, xbc, dt_raw, conv_w, conv_b, dt_bias, a_log, d_skip, gate_norm, bsz, seq, tb=512):
    m, c_inner = z.shape
    nh = dt_raw.shape[1]
    hpg = nh // C_GROUPS
    width = c_inner // C_GROUPS
    assert width == hpg * C_HEAD_DIM and width % LANES == 0 and C_STATE % LANES == 0 and hpg % 2 == 0
    steps = seq // tb
    dt = jax.nn.softplus(dt_raw.astype(F32) + dt_bias.astype(F32)).reshape(bsz, seq, C_GROUPS, hpg)
    da = dt * -jnp.exp(a_log.astype(F32)).reshape(C_GROUPS, hpg)
    dt_g = jnp.transpose(dt, (0, 2, 1, 3))
    da_g = jnp.transpose(da, (0, 2, 1, 3))
    dt_t = jnp.transpose(dt, (0, 2, 3, 1))
    da_t = jnp.transpose(da, (0, 2, 3, 1))
    dskip = jnp.repeat(d_skip.astype(F32), C_HEAD_DIM).reshape(C_GROUPS, 1, width)
    gn = gate_norm.astype(F32).reshape(C_GROUPS, 1, width)
    cw = conv_w.astype(F32)
    cbias = conv_b.astype(F32).reshape(1, -1)
    boff = c_inner // C_STATE
    coff = boff + C_GROUPS

    row_ix = lambda b, s: b * steps + s
    small = pl.BlockSpec((1, 1, tb, hpg), lambda b, g, s: (b, g, s, 0))
    small_t = pl.BlockSpec((1, 1, hpg, tb), lambda b, g, s: (b, g, 0, s))
    per_group = pl.BlockSpec((None, 1, width), lambda b, g, s: (g, 0, 0))
    in_specs = [
        pl.BlockSpec((tb, width), lambda b, g, s: (row_ix(b, s), g)),
        pl.BlockSpec((tb, C_STATE), lambda b, g, s: (row_ix(b, s), boff + g)),
        pl.BlockSpec((tb, C_STATE), lambda b, g, s: (row_ix(b, s), coff + g)),
        pl.BlockSpec((tb, width), lambda b, g, s: (row_ix(b, s), g)),
        pl.BlockSpec((C_CONV, width), lambda b, g, s: (0, g)),
        pl.BlockSpec((C_CONV, C_STATE), lambda b, g, s: (0, boff + g)),
        pl.BlockSpec((C_CONV, C_STATE), lambda b, g, s: (0, coff + g)),
        pl.BlockSpec((1, width), lambda b, g, s: (0, g)),
        pl.BlockSpec((1, C_STATE), lambda b, g, s: (0, boff + g)),
        pl.BlockSpec((1, C_STATE), lambda b, g, s: (0, coff + g)),
        small, small_t, small, small_t, per_group, per_group,
    ]
    return pl.pallas_call(
        functools.partial(_ssd_kernel, tb=tb, hpg=hpg, hdim=C_HEAD_DIM),
        grid=(bsz, C_GROUPS, steps),
        in_specs=in_specs,
        out_specs=pl.BlockSpec((tb, width), lambda b, g, s: (row_ix(b, s), g)),
        out_shape=jax.ShapeDtypeStruct((m, c_inner), BF16),
        scratch_shapes=[pltpu.VMEM((C_STATE, width), F32), pltpu.VMEM((CONV_TAIL, width), F32),
                        pltpu.VMEM((CONV_TAIL, C_STATE), F32), pltpu.VMEM((CONV_TAIL, C_STATE), F32)],
        compiler_params=pltpu.CompilerParams(
            dimension_semantics=("parallel", "parallel", "arbitrary"), vmem_limit_bytes=VMEM_LIMIT_BYTES),
        name="ssd",
    )(xbc, xbc, xbc, z, cw, cw, cw, cbias, cbias, cbias, dt_g, dt_t, da_g, da_t, dskip, gn)


def _dsa_rwkv_layer(x, positions, norm_mix, w_in, q_norm, w_uq, w_qi, kv_norm, w_uk, w_uv,
                    shift_mu, w_up, w0, a_up, a0, g_up, k_k, k_a, r_k, gn_w, gn_b, w_out,
                    norm_ffn, ffn_gate, ffn_up, ffn_down):
    bsz, seq, d = x.shape
    m = bsz * seq
    x2 = x.reshape(m, d)
    cos_a, sin_a = _rope_tables(positions, A_HEAD_DIM)
    cos_i, sin_i = _rope_tables(positions, IDX_DIM)

    b_in = w_in.shape[1] - A_IN
    b_col = _round_up(A_IN, 8 * B_HEAD_DIM)
    width = _round_up(b_col + b_in, 256)
    w_in_b = w_in.astype(BF16)
    w_in_p = jnp.concatenate([_pad_axis(w_in_b[:, :A_IN], 1, b_col), _pad_axis(w_in_b[:, A_IN:], 1, width - b_col)], axis=1)
    mu_p = jnp.concatenate([jnp.zeros((b_col,), F32), _pad_axis(shift_mu.astype(F32), 0, width - b_col)])
    proj = matmul_shift_pallas(rmsnorm_pallas(x2, norm_mix), w_in_p, mu_p, seq, name="l0_in").reshape(bsz, seq, width)
    c_q = proj[:, :, :A_Q_RANK]
    c_kv = proj[:, :, A_Q_RANK:A_Q_RANK + A_KV_RANK]
    k_idx = proj[:, :, A_Q_RANK + A_KV_RANK:A_Q_RANK + A_KV_RANK + IDX_DIM]
    w_idx = proj[:, :, A_Q_RANK + A_KV_RANK + IDX_DIM:A_IN]
    c_q = _rmsnorm_jax(c_q, q_norm)
    c_kv = _rmsnorm_jax(c_kv, kv_norm)
    c_q2 = c_q.reshape(m, A_Q_RANK).astype(BF16)
    rope_cs = lambda cos, sin: (jnp.concatenate([cos, cos], -1).reshape(m, -1), jnp.concatenate([-sin, sin], -1).reshape(m, -1))
    q = matmul_rope_pallas(c_q2, w_uq.astype(BF16), *rope_cs(cos_a, sin_a), scale=A_HEAD_DIM ** -0.5, name="l0_q")
    q_idx = matmul_rope_pallas(c_q2, w_qi.astype(BF16), *rope_cs(cos_i, sin_i), name="l0_qidx")
    kv = _mm3(c_kv, jnp.concatenate([w_uk, w_uv], axis=1))
    k = _apply_rope(kv[:, :, None, :A_HEAD_DIM], cos_a, sin_a)[:, :, 0]
    v = kv[:, :, A_HEAD_DIM:]
    k_idx = _apply_rope(k_idx[:, :, None, :], cos_i, sin_i)[:, :, 0]
    w_idx = w_idx * (IDX_HEADS * IDX_DIM) ** -0.5
    y_a = dsa_pallas(q.reshape(bsz, seq, -1), q_idx.reshape(bsz, seq, -1), w_idx, k, v, k_idx,
                     topk=min(IDX_TOPK_MAX, seq // 4), ha=A_HEADS, hi=IDX_HEADS)
    y_b = _rwkv7_time_mix(proj, b_col, w_up, w0, a_up, a0, g_up, k_k, k_a, r_k, gn_w, gn_b)
    y = jnp.concatenate([y_a.astype(BF16), y_b], axis=-1).reshape(m, -1)
    x2 = matmul_pallas(y, w_out.astype(BF16), res=x2, name="l0_out")

    ffn = ffn_gate.shape[1]
    ffn_p = _round_up(ffn, 512)
    h = rmsnorm_pallas(x2, norm_ffn)
    wg = _pad_axis(ffn_gate.astype(BF16), 1, ffn_p)[None]
    wu = _pad_axis(ffn_up.astype(BF16), 1, ffn_p)[None]
    wd = _pad_axis(ffn_down.astype(BF16), 0, ffn_p)
    tm = min(1024, m)
    hh = grouped_glu_pallas(h, wg, wu, jnp.zeros((m // tm,), I32), jnp.full((1,), m // tm, I32), tm=tm, name="l0_glu")
    x2 = matmul_pallas(hh, wd, res=x2, tm=1024, tn=1024, tk=ffn_p // 4, name="l0_down")
    return x2.reshape(bsz, seq, d)


def _ssd_moe_layer(x, norm_mix, w_in, conv_w, conv_b, dt_bias, a_log, d_skip, gate_norm, w_out,
                   norm_ffn, router, exp_gate, exp_up, exp_down):
    bsz, seq, d = x.shape
    m = bsz * seq
    x2 = x.reshape(m, d)
    c_inner = w_out.shape[0]
    c_conv_dim = c_inner + 2 * C_GROUPS * C_STATE
    h = rmsnorm_pallas(x2, norm_mix)
    w_in_b = w_in.astype(BF16)
    z = matmul_pallas(h, w_in_b[:, :c_inner], name="l1_in_z")
    xbc = matmul_pallas(h, w_in_b[:, c_inner:c_inner + c_conv_dim], name="l1_in_xbc")
    dt_raw = matmul_pallas(h, w_in_b[:, c_inner + c_conv_dim:], name="l1_in_dt")
    y = ssd_mixer_pallas(z, xbc, dt_raw, conv_w, conv_b, dt_bias, a_log, d_skip, gate_norm, bsz, seq)
    x2 = matmul_pallas(y, w_out.astype(BF16), res=x2, tm=512, tn=512, name="l1_out")
    return (x2,) + moe_sparse(x2, norm_ffn, router, exp_gate, exp_up, exp_down)


def kernel(x, positions, l0_norm_mix, l0_w_in, l0_q_norm, l0_w_uq, l0_w_qi, l0_kv_norm, l0_w_uk, l0_w_uv,
           l0_shift_mu, l0_w_up, l0_w0, l0_a_up, l0_a0, l0_g_up, l0_k_k, l0_k_a, l0_r_k, l0_gn_w, l0_gn_b,
           l0_w_out, l0_norm_ffn, l0_ffn_gate, l0_ffn_up, l0_ffn_down,
           l1_norm_mix, l1_w_in, l1_conv_w, l1_conv_b, l1_dt_bias, l1_a_log, l1_d_skip, l1_gate_norm, l1_w_out,
           l1_norm_ffn, l1_router, l1_exp_gate, l1_exp_up, l1_exp_down, final_norm):
    x = _dsa_rwkv_layer(x, positions, l0_norm_mix, l0_w_in, l0_q_norm, l0_w_uq, l0_w_qi, l0_kv_norm, l0_w_uk,
                        l0_w_uv, l0_shift_mu, l0_w_up, l0_w0, l0_a_up, l0_a0, l0_g_up, l0_k_k, l0_k_a, l0_r_k,
                        l0_gn_w, l0_gn_b, l0_w_out, l0_norm_ffn, l0_ffn_gate, l0_ffn_up, l0_ffn_down)
    bsz, seq, d = x.shape
    x2, y1, y2, route = _ssd_moe_layer(x, l1_norm_mix, l1_w_in, l1_conv_w, l1_conv_b, l1_dt_bias, l1_a_log, l1_d_skip,
                                       l1_gate_norm, l1_w_out, l1_norm_ffn, l1_router, l1_exp_gate, l1_exp_up,
                                       l1_exp_down)
    return combine_norm_pallas(x2, y1, y2, route, final_norm).reshape(bsz, seq, d)
```

```python
import functools

import numpy as np
import jax
import jax.numpy as jnp
from jax import lax
from jax.experimental import pallas as pl
from jax.experimental.pallas import tpu as pltpu

F32 = jnp.float32
BF16 = jnp.bfloat16
I32 = jnp.int32
HI = lax.Precision.HIGHEST

CHUNK = 64
QBLOCK = 128
ROPE_THETA = 10000.0
NORM_EPS = 1e-6

A_HEADS = 16
A_HEAD_DIM = 128
A_Q_RANK = 1024
A_KV_RANK = 512
IDX_HEADS = 32
IDX_DIM = 128
IDX_TOPK_MAX = 256

B_HEADS = 32
B_HEAD_DIM = 64
B_WIDTH = B_HEADS * B_HEAD_DIM
B_DECAY_RANK = 128
B_ICLR_RANK = 128
B_GATE_RANK = 480
B_GN_EPS = 64e-5
A_IN = A_Q_RANK + A_KV_RANK + IDX_DIM + IDX_HEADS

C_HEAD_DIM = 64
C_GROUPS = 8
C_STATE = 128
C_CONV = 4

TOP_K_EXPERTS = 2

LANES = 128
VMEM_LIMIT_BYTES = 56 * 1024 * 1024


def _round_up(n, m):
    return (n + m - 1) // m * m


def _pad_axis(a, axis, size):
    if a.shape[axis] == size:
        return a
    pads = [(0, 0)] * a.ndim
    pads[axis] = (0, size - a.shape[axis])
    return jnp.pad(a, pads)


def _rmsnorm_kernel(x_ref, g_ref, o_ref):
    x = x_ref[...].astype(F32)
    ms = jnp.mean(x * x, axis=-1, keepdims=True)
    o_ref[...] = (x * lax.rsqrt(ms + NORM_EPS) * g_ref[...]).astype(o_ref.dtype)


def rmsnorm_pallas(x, g, out_dtype=BF16, tm=256):
    m, d = x.shape
    tm = min(tm, m)
    return pl.pallas_call(
        _rmsnorm_kernel,
        grid=(m // tm,),
        in_specs=[pl.BlockSpec((tm, d), lambda i: (i, 0)),
                  pl.BlockSpec((1, d), lambda i: (0, 0))],
        out_specs=pl.BlockSpec((tm, d), lambda i: (i, 0)),
        out_shape=jax.ShapeDtypeStruct((m, d), out_dtype),
        compiler_params=pltpu.CompilerParams(dimension_semantics=("parallel",)),
        name="rmsnorm",
    )(x, g.reshape(1, d).astype(F32))


def _mm_kernel(x_ref, w_ref, *rest, nk, has_res):
    if has_res:
        res_ref, o_ref, acc_ref = rest
    else:
        o_ref, acc_ref = rest
        res_ref = None
    p = jnp.dot(x_ref[...], w_ref[...], preferred_element_type=F32)

    def finish(acc):
        if has_res:
            acc = acc + res_ref[...].astype(F32)
        o_ref[...] = acc.astype(o_ref.dtype)

    if nk == 1:
        finish(p)
        return
    k = pl.program_id(2)

    @pl.when(k == 0)
    def _():
        acc_ref[...] = p

    @pl.when(jnp.logical_and(k > 0, k < nk - 1))
    def _():
        acc_ref[...] += p

    @pl.when(k == nk - 1)
    def _():
        finish(acc_ref[...] + p)


def matmul_pallas(x, w, res=None, out_dtype=F32, tm=1024, tn=512, tk=None, name="matmul"):
    m, kdim = x.shape
    k2, n = w.shape
    assert kdim == k2
    tm = min(tm, m)
    tn = min(tn, n)
    tk = kdim if tk is None else min(tk, kdim)
    assert m % tm == 0 and n % tn == 0 and kdim % tk == 0, (x.shape, w.shape, tm, tn, tk)
    nk = kdim // tk
    in_specs = [pl.BlockSpec((tm, tk), lambda i, j, k: (i, k)),
                pl.BlockSpec((tk, tn), lambda i, j, k: (k, j))]
    args = [x, w]
    if res is not None:
        in_specs.append(pl.BlockSpec((tm, tn), lambda i, j, k: (i, j)))
        args.append(res)
    acc_shape = (tm, tn) if nk > 1 else (8, 128)
    return pl.pallas_call(
        functools.partial(_mm_kernel, nk=nk, has_res=res is not None),
        grid=(m // tm, n // tn, nk),
        in_specs=in_specs,
        out_specs=pl.BlockSpec((tm, tn), lambda i, j, k: (i, j)),
        out_shape=jax.ShapeDtypeStruct((m, n), out_dtype),
        scratch_shapes=[pltpu.VMEM(acc_shape, F32)],
        compiler_params=pltpu.CompilerParams(
            dimension_semantics=("parallel", "parallel", "arbitrary"),
            vmem_limit_bytes=VMEM_LIMIT_BYTES),
        name=name,
    )(*args)


def _mm_rope_kernel(x_ref, w_ref, cos_ref, sin_ref, o_ref, *, scale, head_dim):
    p = jnp.dot(x_ref[...], w_ref[...], preferred_element_type=F32) * scale
    cos = cos_ref[...]
    sin = sin_ref[...]
    heads = []
    for h in range(p.shape[1] // head_dim):
        ph = p[:, h * head_dim:(h + 1) * head_dim]
        heads.append(ph * cos + pltpu.roll(ph, head_dim // 2, 1) * sin)
    o_ref[...] = jnp.concatenate(heads, axis=1).astype(o_ref.dtype)


def matmul_rope_pallas(x, w, cos_full, sin_signed, scale=1.0, tm=1024, tn=512, name="matmul_rope"):
    m, kdim = x.shape
    n = w.shape[1]
    hd = cos_full.shape[1]
    tm = min(tm, m)
    assert m % tm == 0 and n % tn == 0 and tn % hd == 0
    return pl.pallas_call(
        functools.partial(_mm_rope_kernel, scale=scale, head_dim=hd),
        grid=(m // tm, n // tn),
        in_specs=[pl.BlockSpec((tm, kdim), lambda i, j: (i, 0)),
                  pl.BlockSpec((kdim, tn), lambda i, j: (0, j)),
                  pl.BlockSpec((tm, hd), lambda i, j: (i, 0)),
                  pl.BlockSpec((tm, hd), lambda i, j: (i, 0))],
        out_specs=pl.BlockSpec((tm, tn), lambda i, j: (i, j)),
        out_shape=jax.ShapeDtypeStruct((m, n), BF16),
        compiler_params=pltpu.CompilerParams(
            dimension_semantics=("parallel", "parallel"), vmem_limit_bytes=VMEM_LIMIT_BYTES),
        name=name,
    )(x, w, cos_full, sin_signed)


SHIFT_TAIL = 8


def _mm_shift_kernel(x_ref, w_ref, mu_ref, o_ref, carry_ref, *, tiles_per_seq):
    i = pl.program_id(0)
    j = pl.program_id(1)

    @pl.when(i == 0)
    def _():
        carry_ref[j] = jnp.zeros(carry_ref.shape[1:], F32)

    p = jnp.dot(x_ref[...], w_ref[...], preferred_element_type=F32)
    tm = p.shape[0]
    tail = jnp.where(i % tiles_per_seq == 0, 0.0, carry_ref[j])
    prev = jnp.concatenate([tail, p], axis=0)[SHIFT_TAIL - 1:SHIFT_TAIL - 1 + tm]
    carry_ref[j] = p[tm - SHIFT_TAIL:]
    o_ref[...] = p + (prev - p) * mu_ref[...]


def matmul_shift_pallas(x, w, mu, seq, tm=1024, tn=256, name="matmul_shift"):
    m, kdim = x.shape
    n = w.shape[1]
    tm = min(tm, seq)
    assert seq % tm == 0 and m % seq == 0 and n % tn == 0
    return pl.pallas_call(
        functools.partial(_mm_shift_kernel, tiles_per_seq=seq // tm),
        grid=(m // tm, n // tn),
        in_specs=[pl.BlockSpec((tm, kdim), lambda i, j: (i, 0)),
                  pl.BlockSpec((kdim, tn), lambda i, j: (0, j)),
                  pl.BlockSpec((1, tn), lambda i, j: (0, j))],
        out_specs=pl.BlockSpec((tm, tn), lambda i, j: (i, j)),
        out_shape=jax.ShapeDtypeStruct((m, n), F32),
        scratch_shapes=[pltpu.VMEM((n // tn, SHIFT_TAIL, tn), F32)],
        compiler_params=pltpu.CompilerParams(
            dimension_semantics=("arbitrary", "arbitrary"), vmem_limit_bytes=VMEM_LIMIT_BYTES),
        name=name,
    )(x, w, mu.reshape(1, n).astype(F32))


def _gglu_kernel(te_ref, nu_ref, x_ref, wg_ref, wu_ref, o_ref, *, row_axis):
    t = pl.program_id(row_axis)

    @pl.when(t < nu_ref[0])
    def _():
        x = x_ref[...]
        g = jnp.dot(x, wg_ref[...].astype(BF16), preferred_element_type=F32)
        u = jnp.dot(x, wu_ref[...].astype(BF16), preferred_element_type=F32)
        o_ref[...] = (g * jax.nn.sigmoid(g) * u).astype(o_ref.dtype)

    @pl.when(t >= nu_ref[0])
    def _():
        o_ref[...] = jnp.zeros_like(o_ref)


def _grouped_grid(r, n, tm, tn, weights_outer):
    if weights_outer:
        return (n // tn, r // tm), 1, lambda f: (lambda j, t, te, nu: f(t, j, te))
    return (r // tm, n // tn), 0, lambda f: (lambda t, j, te, nu: f(t, j, te))


def grouped_glu_pallas(xs, wg, wu, tile_expert, n_used, tm, tn=512, weights_outer=False, name="glu"):
    r, kdim = xs.shape
    n = wg.shape[2]
    assert r % tm == 0 and n % tn == 0
    grid, row_axis, ix = _grouped_grid(r, n, tm, tn, weights_outer)
    grid_spec = pltpu.PrefetchScalarGridSpec(
        num_scalar_prefetch=2,
        grid=grid,
        in_specs=[pl.BlockSpec((tm, kdim), ix(lambda t, j, te: (t, 0))),
                  pl.BlockSpec((None, kdim, tn), ix(lambda t, j, te: (te[t], 0, j))),
                  pl.BlockSpec((None, kdim, tn), ix(lambda t, j, te: (te[t], 0, j)))],
        out_specs=pl.BlockSpec((tm, tn), ix(lambda t, j, te: (t, j))),
    )
    return pl.pallas_call(
        functools.partial(_gglu_kernel, row_axis=row_axis),
        grid_spec=grid_spec,
        out_shape=jax.ShapeDtypeStruct((r, n), BF16),
        compiler_params=pltpu.CompilerParams(
            dimension_semantics=("arbitrary", "arbitrary"), vmem_limit_bytes=VMEM_LIMIT_BYTES),
        name=name,
    )(tile_expert, n_used, xs, wg, wu)


def _gdown_kernel(te_ref, nu_ref, x_ref, w_ref, o_ref, *, row_axis):
    t = pl.program_id(row_axis)

    @pl.when(t < nu_ref[0])
    def _():
        o_ref[...] = jnp.dot(x_ref[...], w_ref[...].astype(BF16), preferred_element_type=F32).astype(o_ref.dtype)

    @pl.when(t >= nu_ref[0])
    def _():
        o_ref[...] = jnp.zeros_like(o_ref)


def grouped_down_pallas(hs, wd, tile_expert, n_used, tm, tn=512, weights_outer=False, name="down"):
    r, kdim = hs.shape
    n = wd.shape[2]
    assert r % tm == 0 and n % tn == 0
    grid, row_axis, ix = _grouped_grid(r, n, tm, tn, weights_outer)
    grid_spec = pltpu.PrefetchScalarGridSpec(
        num_scalar_prefetch=2,
        grid=grid,
        in_specs=[pl.BlockSpec((tm, kdim), ix(lambda t, j, te: (t, 0))),
                  pl.BlockSpec((None, kdim, tn), ix(lambda t, j, te: (te[t], 0, j)))],
        out_specs=pl.BlockSpec((tm, tn), ix(lambda t, j, te: (t, j))),
    )
    return pl.pallas_call(
        functools.partial(_gdown_kernel, row_axis=row_axis),
        grid_spec=grid_spec,
        out_shape=jax.ShapeDtypeStruct((r, n), BF16),
        compiler_params=pltpu.CompilerParams(
            dimension_semantics=("arbitrary", "arbitrary"), vmem_limit_bytes=VMEM_LIMIT_BYTES),
        name=name,
    )(tile_expert, n_used, hs, wd)


def _router_kernel(x_ref, g_ref, wr_ref, h_ref, r_ref, *, n_experts):
    x = x_ref[...]
    h = x * lax.rsqrt(jnp.mean(x * x, axis=-1, keepdims=True) + NORM_EPS) * g_ref[...]
    h_ref[...] = h.astype(h_ref.dtype)
    logits = jnp.dot(h, wr_ref[...], preferred_element_type=F32, precision=HI)
    lane = lax.broadcasted_iota(I32, logits.shape, 1)
    logits = jnp.where(lane < n_experts, logits, -jnp.inf)
    m1 = jnp.max(logits, axis=1, keepdims=True)
    i1 = jnp.min(jnp.where(logits == m1, lane, LANES), axis=1, keepdims=True)
    rest = jnp.where(lane == i1, -jnp.inf, logits)
    m2 = jnp.max(rest, axis=1, keepdims=True)
    i2 = jnp.min(jnp.where(rest == m2, lane, LANES), axis=1, keepdims=True)
    e2 = jnp.exp(m2 - m1)
    w1 = 1.0 / (1.0 + e2)
    w2 = e2 / (1.0 + e2)
    out = jnp.where(lane == 0, i1.astype(F32), 0.0)
    out = jnp.where(lane == 1, i2.astype(F32), out)
    out = jnp.where(lane == 2, w1, out)
    out = jnp.where(lane == 3, w2, out)
    r_ref[...] = out


def router_pallas(x, g, router, tm=256):
    m, d = x.shape
    ne = router.shape[1]
    wr = jnp.pad(router.astype(F32), ((0, 0), (0, LANES - ne)))
    return pl.pallas_call(
        functools.partial(_router_kernel, n_experts=ne),
        grid=(m // tm,),
        in_specs=[pl.BlockSpec((tm, d), lambda i: (i, 0)),
                  pl.BlockSpec((1, d), lambda i: (0, 0)),
                  pl.BlockSpec((d, LANES), lambda i: (0, 0))],
        out_specs=[pl.BlockSpec((tm, d), lambda i: (i, 0)),
                   pl.BlockSpec((tm, LANES), lambda i: (i, 0))],
        out_shape=[jax.ShapeDtypeStruct((m, d), BF16), jax.ShapeDtypeStruct((m, LANES), F32)],
        compiler_params=pltpu.CompilerParams(dimension_semantics=("parallel",)),
        name="router",
    )(x, g.reshape(1, d).astype(F32), wr)


def moe_sparse(x2, norm_ffn, router, exp_gate, exp_up, exp_down, tm=512):
    m, d = x2.shape
    ne = router.shape[1]
    h, route = router_pallas(x2, norm_ffn, router)
    idx = route[:, :TOP_K_EXPERTS].astype(I32)
    flat_e = idx.reshape(-1)
    onehot = (flat_e[:, None] == jnp.arange(ne, dtype=I32)[None, :]).astype(I32)
    rank = jnp.sum((jnp.cumsum(onehot, axis=0) - 1) * onehot, axis=1)
    count = jnp.sum(onehot, axis=0)
    padded = (count + tm - 1) // tm * tm
    ends = jnp.cumsum(padded)
    pos = (ends - padded)[flat_e] + rank
    rows = m * TOP_K_EXPERTS + ne * tm
    n_tiles = rows // tm
    token_of_row = jnp.zeros((rows,), I32).at[pos].set(jnp.arange(m * TOP_K_EXPERTS, dtype=I32) // TOP_K_EXPERTS)
    tile_start = jnp.arange(n_tiles, dtype=I32)[:, None] * tm
    tile_expert = jnp.minimum(jnp.sum((tile_start >= ends[None, :]).astype(I32), axis=1), ne - 1).astype(I32)
    n_used = (ends[-1] // tm).astype(I32).reshape(1)
    xs = jnp.take(h, token_of_row, axis=0)
    hh = grouped_glu_pallas(xs, exp_gate, exp_up, tile_expert, n_used, tm, weights_outer=True, name="moe_glu")
    ys = grouped_down_pallas(hh, exp_down, tile_expert, n_used, tm, weights_outer=True, name="moe_down")
    pos2 = pos.reshape(m, TOP_K_EXPERTS)
    return jnp.take(ys, pos2[:, 0], axis=0), jnp.take(ys, pos2[:, 1], axis=0), route


def _combine_norm_kernel(x_ref, y1_ref, y2_ref, r_ref, g_ref, o_ref):
    r = r_ref[...]
    w1 = r[:, TOP_K_EXPERTS:TOP_K_EXPERTS + 1]
    w2 = r[:, TOP_K_EXPERTS + 1:TOP_K_EXPERTS + 2]
    x = x_ref[...] + y1_ref[...].astype(F32) * w1 + y2_ref[...].astype(F32) * w2
    ms = jnp.mean(x * x, axis=-1, keepdims=True)
    o_ref[...] = (x * lax.rsqrt(ms + NORM_EPS) * g_ref[...]).astype(o_ref.dtype)


def combine_norm_pallas(x, y1, y2, route, g, tm=256):
    m, d = x.shape
    row = pl.BlockSpec((tm, d), lambda i: (i, 0))
    return pl.pallas_call(
        _combine_norm_kernel,
        grid=(m // tm,),
        in_specs=[row, row, row, pl.BlockSpec((tm, LANES), lambda i: (i, 0)), pl.BlockSpec((1, d), lambda i: (0, 0))],
        out_specs=row,
        out_shape=jax.ShapeDtypeStruct((m, d), x.dtype),
        compiler_params=pltpu.CompilerParams(dimension_semantics=("parallel",)),
        name="combine_norm",
    )(x, y1, y2, route, g.reshape(1, d).astype(F32))


def _rmsnorm_jax(x, g):
    xf = x.astype(F32)
    return xf * lax.rsqrt(jnp.mean(xf * xf, axis=-1, keepdims=True) + NORM_EPS) * g.astype(F32)


def _rope_tables(positions, dim):
    inv_freq = ROPE_THETA ** (-jnp.arange(0, dim, 2, dtype=F32) / dim)
    ang = positions.astype(F32)[..., None] * inv_freq
    return jnp.cos(ang)[:, :, None, :], jnp.sin(ang)[:, :, None, :]


def _apply_rope(x, cos, sin):
    x1, x2 = jnp.split(x.astype(F32), 2, axis=-1)
    return jnp.concatenate([x1 * cos - x2 * sin, x2 * cos + x1 * sin], axis=-1)


NEG_BIG = -1e30
INT_MIN = np.int32(-2 ** 31)


def _dsa_kernel(qi_ref, w_ref, q_ref, kit_ref, kt_ref, v_ref, o_ref,
                s_ref, wb_ref, m_ref, l_ref, acc_ref, qis_ref, qs_ref, shi_ref, slo_ref, *, tk, idx_heads, att_heads, topk, head_group):
    i = pl.program_id(1)
    hd_i = qi_ref.shape[2] // idx_heads
    for h in range(idx_heads):
        qis_ref[h * QBLOCK:(h + 1) * QBLOCK, :] = qi_ref[0, :, h * hd_i:(h + 1) * hd_i]
    hd_a = q_ref.shape[2] // att_heads
    for h in range(att_heads):
        qs_ref[h * QBLOCK:(h + 1) * QBLOCK, :] = q_ref[0, :, h * hd_a:(h + 1) * hd_a]
    n_tiles = ((i + 1) * QBLOCK + tk - 1) // tk
    nl = tk // LANES
    row = lax.broadcasted_iota(I32, (QBLOCK, tk), 0)
    col = lax.broadcasted_iota(I32, (QBLOCK, tk), 1)
    limit = (2 * i + 1 + (row >= CHUNK).astype(I32)) * CHUNK

    def admissible(t):
        return (t * tk + col) < limit

    def fold_lanes(x, op):
        out = x[:, :LANES]
        for j in range(1, nl):
            out = op(out, x[:, j * LANES:(j + 1) * LANES])
        return out

    w = w_ref[0, 0]
    for h in range(idx_heads):
        wb_ref[h] = jnp.broadcast_to(w[:, h:h + 1], (QBLOCK, LANES))

    def idx_body(t, carry):
        kit = kit_ref[0, t]
        acc = jnp.zeros((QBLOCK, tk), F32)
        for g in range(idx_heads // head_group):
            rows = head_group * QBLOCK
            rel = jnp.dot(qis_ref[g * rows:(g + 1) * rows, :], kit, preferred_element_type=F32)
            for hh in range(head_group):
                h = g * head_group + hh
                relu = jnp.maximum(rel[hh * QBLOCK:(hh + 1) * QBLOCK], 0.0)
                acc = acc + jnp.concatenate([wb_ref[h]] * nl, axis=1) * relu
        score = jnp.where(admissible(t), acc, -jnp.inf)
        bits = lax.bitcast_convert_type(score, I32)
        key = bits ^ ((bits >> 31) & np.int32(0x7FFFFFFF))
        s_ref[t] = key
        shi_ref[t] = (key >> 16).astype(jnp.int16)
        return carry

    lax.fori_loop(0, n_tiles, idx_body, 0)

    I16 = jnp.int16

    def search16(ref, need):
        def bit_body(it, ans):
            cand = ans | lax.shift_left(np.int32(1), 15 - it)
            cand_s = (cand - 32768).astype(I16)

            def cnt_body(t, cnt):
                return cnt + fold_lanes((ref[t] >= cand_s).astype(I16), jnp.add)

            cnt = lax.fori_loop(0, n_tiles, cnt_body, jnp.zeros((QBLOCK, LANES), I16))
            total = jnp.sum(cnt.astype(I32), axis=1, keepdims=True)
            return jnp.where(total >= need, cand, ans)

        return lax.fori_loop(0, 16, bit_body, jnp.zeros((QBLOCK, 1), I32))

    ans_hi = search16(shi_ref, topk)
    hi_star = (ans_hi - 32768).astype(I16)

    def low_body(t, cnt):
        hi = shi_ref[t]
        lo = ((s_ref[t] & np.int32(0xFFFF)) - 32768).astype(I16)
        slo_ref[t] = jnp.where(hi == hi_star, lo, np.int16(-32768))
        return cnt + fold_lanes((hi > hi_star).astype(I16), jnp.add)

    cnt_gt = lax.fori_loop(0, n_tiles, low_body, jnp.zeros((QBLOCK, LANES), I16))
    need_lo = topk - jnp.sum(cnt_gt.astype(I32), axis=1, keepdims=True)
    ans_lo = search16(slo_ref, need_lo)
    thr = (lax.shift_left(ans_hi, 16) | ans_lo) ^ INT_MIN

    def masked_logits(t):
        sel = jnp.logical_and(s_ref[t] >= thr, admissible(t))
        bias = jnp.where(sel, 0.0, NEG_BIG)
        logits = jnp.dot(qs_ref[...], kt_ref[0, t], preferred_element_type=F32)
        return logits + jnp.concatenate([bias] * att_heads, axis=0)

    m_ref[...] = jnp.full_like(m_ref, NEG_BIG)

    def max_body(t, carry):
        m_ref[...] = jnp.maximum(m_ref[...], fold_lanes(masked_logits(t), jnp.maximum))
        return carry

    lax.fori_loop(0, n_tiles, max_body, 0)
    m_ref[...] = jnp.broadcast_to(jnp.max(m_ref[...], axis=1, keepdims=True), m_ref.shape)

    l_ref[...] = jnp.zeros_like(l_ref)
    acc_ref[...] = jnp.zeros_like(acc_ref)

    def att_body(t, carry):
        p = jnp.exp(masked_logits(t) - jnp.concatenate([m_ref[...]] * nl, axis=1))
        l_ref[...] += fold_lanes(p, jnp.add)
        acc_ref[...] += jnp.dot(p.astype(BF16), v_ref[0, t], preferred_element_type=F32)
        return carry

    lax.fori_loop(0, n_tiles, att_body, 0)
    out = acc_ref[...] / jnp.sum(l_ref[...], axis=1, keepdims=True)
    d = out.shape[1]
    for h in range(att_heads):
        o_ref[0, :, h * d:(h + 1) * d] = out[h * QBLOCK:(h + 1) * QBLOCK, :]


def dsa_pallas(q, q_idx, w_idx, k, v, k_idx, topk, ha, hi, tk=512, head_group=8):
    bsz, seq, _ = q.shape
    d = q.shape[2] // ha
    di = q_idx.shape[2] // hi
    nb = seq // QBLOCK
    nt = seq // tk
    wb = w_idx.astype(F32).reshape(bsz, nb, QBLOCK, hi)
    kit = jnp.transpose(k_idx.astype(BF16).reshape(bsz, nt, tk, di), (0, 1, 3, 2))
    kt = jnp.transpose(k.astype(BF16).reshape(bsz, nt, tk, d), (0, 1, 3, 2))
    vb = v.astype(BF16).reshape(bsz, nt, tk, d)
    return pl.pallas_call(
        functools.partial(_dsa_kernel, tk=tk, idx_heads=hi, att_heads=ha, topk=topk, head_group=head_group),
        grid=(bsz, nb),
        in_specs=[
            pl.BlockSpec((1, QBLOCK, hi * di), lambda b, i: (b, i, 0)),
            pl.BlockSpec((1, 1, QBLOCK, hi), lambda b, i: (b, i, 0, 0)),
            pl.BlockSpec((1, QBLOCK, ha * d), lambda b, i: (b, i, 0)),
            pl.BlockSpec((1, nt, di, tk), lambda b, i: (b, 0, 0, 0)),
            pl.BlockSpec((1, nt, d, tk), lambda b, i: (b, 0, 0, 0)),
            pl.BlockSpec((1, nt, tk, d), lambda b, i: (b, 0, 0, 0)),
        ],
        out_specs=pl.BlockSpec((1, QBLOCK, ha * d), lambda b, i: (b, i, 0)),
        out_shape=jax.ShapeDtypeStruct((bsz, seq, ha * d), F32),
        scratch_shapes=[
            pltpu.VMEM((nt, QBLOCK, tk), I32),
            pltpu.VMEM((hi, QBLOCK, LANES), F32),
            pltpu.VMEM((ha * QBLOCK, LANES), F32),
            pltpu.VMEM((ha * QBLOCK, LANES), F32),
            pltpu.VMEM((ha * QBLOCK, d), F32),
            pltpu.VMEM((hi * QBLOCK, di), BF16),
            pltpu.VMEM((ha * QBLOCK, d), BF16),
            pltpu.VMEM((nt, QBLOCK, tk), jnp.int16),
            pltpu.VMEM((nt, QBLOCK, tk), jnp.int16),
        ],
        compiler_params=pltpu.CompilerParams(
            dimension_semantics=("parallel", "arbitrary"),
            vmem_limit_bytes=VMEM_LIMIT_BYTES),
        name="dsa",
    )(q_idx, wb, q, kit, kt, vb)


def _bdot(a, b):
    return jnp.dot(a.astype(BF16), b.astype(BF16), preferred_element_type=F32)


def _bdot_nt(a, b):
    return lax.dot_general(a.astype(BF16), b.astype(BF16), (((1,), (1,)), ((), ())), preferred_element_type=F32)


def _bdot_tn(a, b):
    return lax.dot_general(a.astype(BF16), b.astype(BF16), (((0,), (0,)), ((), ())), preferred_element_type=F32)


def _split3(x):
    hi = x.astype(BF16)
    r1 = x - hi.astype(F32)
    mid = r1.astype(BF16)
    return hi, mid, (r1 - mid.astype(F32)).astype(BF16)


def _sel_dot(sel, x):
    hi, mid, lo = _split3(x)
    s = sel.astype(BF16)
    return (jnp.dot(s, hi, preferred_element_type=F32) + jnp.dot(s, mid, preferred_element_type=F32)
            + jnp.dot(s, lo, preferred_element_type=F32))


def _dot_sel(x, sel):
    hi, mid, lo = _split3(x)
    s = sel.astype(BF16)
    return (jnp.dot(hi, s, preferred_element_type=F32) + jnp.dot(mid, s, preferred_element_type=F32)
            + jnp.dot(lo, s, preferred_element_type=F32))


def _wkv7_kernel(r_ref, k_ref, v_ref, a_ref, lw_ref, g_ref, kk_ref, ka_ref, rk_ref, gnw_ref, gnb_ref,
                 y_ref, z_ref, p_ref, *, heads, clen):
    c = pl.program_id(2)
    row = lax.broadcasted_iota(I32, (clen, clen), 0)
    col = lax.broadcasted_iota(I32, (clen, clen), 1)
    strict = row > col
    incl = row >= col
    tri = incl.astype(F32)
    eye = (row == col).astype(F32)

    @pl.when(c == 0)
    def _():
        z_ref[...] = jnp.zeros_like(z_ref)
        p_ref[...] = jnp.ones_like(p_ref)

    hs = range(heads)
    cols = [slice(h * clen, (h + 1) * clen) for h in hs]
    r = [r_ref[0, :, cols[h]] for h in hs]
    k0 = [k_ref[0, :, cols[h]] for h in hs]
    v = [v_ref[0, :, cols[h]] for h in hs]
    ai = [a_ref[0, :, cols[h]] for h in hs]
    lw = [lw_ref[0, :, cols[h]] for h in hs]
    kk = [k0[h] * kk_ref[:, cols[h]] for h in hs]
    kk = [kk[h] * lax.rsqrt(jnp.maximum(jnp.sum(kk[h] * kk[h], axis=-1, keepdims=True), 1e-24)) for h in hs]
    k = [k0[h] * (1.0 + (ai[h] - 1.0) * ka_ref[:, cols[h]]) for h in hs]
    cum = [_sel_dot(tri, lw[h]) for h in hs]
    ecum = [jnp.exp(cum[h]) for h in hs]
    encum = [jnp.exp(-cum[h]) for h in hs]
    rt = [r[h] * ecum[h] for h in hs]
    kt = [k[h] * encum[h] for h in hs]
    bt = [kk[h] * ai[h] * encum[h] for h in hs]
    at = [-kk[h] * jnp.exp(cum[h] - lw[h]) for h in hs]
    x = [jnp.concatenate([at[h], rt[h]], axis=0) for h in hs]
    aab = [_bdot_nt(x[h], bt[h]) for h in hs]
    aak = [_bdot_nt(x[h], kt[h]) for h in hs]
    a_ab = [jnp.where(strict, aab[h][:clen], 0.0) for h in hs]
    a_ak = [jnp.where(strict, aak[h][:clen], 0.0) for h in hs]
    a_rb = [jnp.where(incl, aab[h][clen:], 0.0) for h in hs]
    a_rk = [jnp.where(incl, aak[h][clen:], 0.0) for h in hs]
    xp = a_ab
    t = [eye + a_ab[h] for h in hs]
    n = 2
    while n < clen:
        xp = [_bdot(xp[h], xp[h]) for h in hs]
        t = [t[h] + _bdot(t[h], xp[h]) for h in hs]
        n *= 2
    av = [_bdot(a_ak[h], v[h]) for h in hs]
    uv = [_bdot(t[h], av[h]) for h in hs]
    at2 = [_bdot(t[h], at[h]) for h in hs]
    rp = [rt[h] + _bdot(a_rb[h], at2[h]) for h in hs]
    yv = [_bdot(a_rb[h], uv[h]) + _bdot(a_rk[h], v[h]) for h in hs]
    mp = [_bdot_tn(bt[h], at2[h]) for h in hs]
    gp = [_bdot_tn(jnp.concatenate([bt[h], kt[h]], axis=0), jnp.concatenate([uv[h], v[h]], axis=0)) for h in hs]
    lhs = [jnp.concatenate([rp[h], eye + mp[h]], axis=0) * p_ref[h] for h in hs]
    out = [_bdot(lhs[h], z_ref[h]) for h in hs]
    for h in hs:
        z_ref[h] = out[h][clen:] + gp[h]
        p_ref[h] = ecum[h][clen - 1:clen, :]
    y = [out[h][:clen] + yv[h] for h in hs]
    mean = [jnp.mean(y[h], axis=-1, keepdims=True) for h in hs]
    var = [jnp.mean(jnp.square(y[h] - mean[h]), axis=-1, keepdims=True) for h in hs]
    yn = [(y[h] - mean[h]) * lax.rsqrt(var[h] + B_GN_EPS) * gnw_ref[:, cols[h]] + gnb_ref[:, cols[h]] for h in hs]
    bonus = [jnp.sum(r[h] * k[h] * rk_ref[:, cols[h]], axis=-1, keepdims=True) * v[h] for h in hs]
    y_ref[0] = (jnp.concatenate([yn[h] + bonus[h] for h in hs], axis=1) * g_ref[0]).astype(y_ref.dtype)


def wkv7_pallas(proj, rkv_col, a, lw, g, k_k, k_a, r_k, gn_w, gn_b, heads_per_step=32, clen=CHUNK):
    bsz, seq, width = a.shape
    nh = width // clen
    gh = min(heads_per_step, nh)
    bw = gh * clen
    assert width % bw == 0 and rkv_col % bw == 0
    nblk = width // bw

    def col_spec(first_block):
        return pl.BlockSpec((1, clen, bw), lambda bi, hi, ci: (bi, ci, first_block + hi))

    par_spec = pl.BlockSpec((1, bw), lambda bi, hi, ci: (0, hi))
    par = lambda t: t.astype(F32).reshape(1, width)
    r0 = rkv_col // bw
    return pl.pallas_call(
        functools.partial(_wkv7_kernel, heads=gh, clen=clen),
        grid=(bsz, nblk, seq // clen),
        in_specs=[col_spec(r0), col_spec(r0 + nblk), col_spec(r0 + 2 * nblk),
                  col_spec(0), col_spec(0), col_spec(0)] + [par_spec] * 5,
        out_specs=col_spec(0),
        out_shape=jax.ShapeDtypeStruct((bsz, seq, width), BF16),
        scratch_shapes=[pltpu.VMEM((gh, clen, clen), F32), pltpu.VMEM((gh, 1, clen), F32)],
        compiler_params=pltpu.CompilerParams(dimension_semantics=("parallel", "parallel", "arbitrary")),
        name="wkv7",
    )(proj, proj, proj, a, lw, g, par(k_k), par(k_a), par(r_k), par(gn_w), par(gn_b))


def _mm3(x, w, **kw):
    bsz, seq, kdim = x.shape
    kp = _round_up(kdim, 128)
    n = w.shape[1]
    np_ = _round_up(n, 128)
    xb = _pad_axis(x.reshape(bsz * seq, kdim).astype(BF16), 1, kp)
    wb = _pad_axis(_pad_axis(w.astype(BF16), 0, kp), 1, np_)
    tn = kw.pop("tn", 512)
    while np_ % tn:
        tn //= 2
    out = matmul_pallas(xb, wb, tn=tn, **kw)
    return out[:, :n].reshape(bsz, seq, n)


def _rwkv7_time_mix(proj, b_col, w_up, w0, a_up, a0, g_up, k_k, k_a, r_k, gn_w, gn_b):
    lo = b_col + 3 * B_WIDTH
    xw = proj[:, :, lo:lo + B_DECAY_RANK]
    xa = proj[:, :, lo + B_DECAY_RANK:lo + B_DECAY_RANK + B_ICLR_RANK]
    xg = proj[:, :, lo + B_DECAY_RANK + B_ICLR_RANK:lo + B_DECAY_RANK + B_ICLR_RANK + B_GATE_RANK]
    w_log = -jax.nn.softplus(-(w0 + _mm3(jnp.tanh(xw), w_up))) - 0.5
    a = jax.nn.sigmoid(a0 + _mm3(xa, a_up))
    g = _mm3(jax.nn.sigmoid(xg), g_up)
    return wkv7_pallas(proj, b_col, a, -jnp.exp(w_log), g, k_k, k_a, r_k, gn_w, gn_b)


CONV_TAIL = 8


def _ssd_kernel(x_ref, bm_ref, cm_ref, z_ref, wx_ref, wb_ref, wc_ref, bx_ref, bb_ref, bc_ref,
                dt_ref, dtt_ref, da_ref, dat_ref, dskip_ref, gn_ref, o_ref,
                state_ref, tx_ref, tb_ref, tc_ref, *, tb, hpg, hdim):
    step = pl.program_id(2)
    width = hpg * hdim

    @pl.when(step == 0)
    def _():
        state_ref[...] = jnp.zeros_like(state_ref)
        tx_ref[...] = jnp.zeros_like(tx_ref)
        tb_ref[...] = jnp.zeros_like(tb_ref)
        tc_ref[...] = jnp.zeros_like(tc_ref)

    def conv_silu(cur_ref, tail_ref, w_ref, b_ref):
        cur = cur_ref[...]
        ext = jnp.concatenate([tail_ref[...], cur], axis=0)
        w = w_ref[...]
        acc = b_ref[...]
        for j in range(C_CONV):
            off = CONV_TAIL - (C_CONV - 1) + j
            acc = acc + w[j:j + 1, :] * ext[off:off + tb, :]
        tail_ref[...] = cur[tb - CONV_TAIL:, :]
        return acc * jax.nn.sigmoid(acc)

    xs = conv_silu(x_ref, tx_ref, wx_ref, bx_ref)
    bm = conv_silu(bm_ref, tb_ref, wb_ref, bb_ref)
    cm = conv_silu(cm_ref, tc_ref, wc_ref, bc_ref)

    row = lax.broadcasted_iota(I32, (CHUNK, CHUNK), 0)
    col = lax.broadcasted_iota(I32, (CHUNK, CHUNK), 1)
    causal = row >= col
    tri = causal.astype(F32)
    triu = (row <= col).astype(F32)
    eh = (lax.broadcasted_iota(I32, (hpg, width), 1) // hdim == lax.broadcasted_iota(I32, (hpg, width), 0)).astype(F32)
    lane_in_pair = lax.broadcasted_iota(I32, (CHUNK, 2 * hdim), 1)

    ys = []
    for c in range(tb // CHUNK):
        lo = c * CHUNK
        xc = xs[lo:lo + CHUNK]
        bc = bm[lo:lo + CHUNK]
        cc = cm[lo:lo + CHUNK]
        dt = dt_ref[0, 0, lo:lo + CHUNK, :]
        dtt = dtt_ref[0, 0, :, lo:lo + CHUNK]
        cum = _sel_dot(tri, da_ref[0, 0, lo:lo + CHUNK, :])
        cumt = _dot_sel(dat_ref[0, 0, :, lo:lo + CHUNK], triu)
        cb = _bdot_nt(cc, bc)
        pairs = []
        for hp in range(hpg // 2):
            xp = xc[:, hp * 2 * hdim:(hp + 1) * 2 * hdim]
            outs = []
            for h in (2 * hp, 2 * hp + 1):
                seg = cum[:, h:h + 1] - cumt[h:h + 1, :]
                wts = cb * jnp.exp(jnp.where(causal, seg, -jnp.inf)) * dtt[h:h + 1, :]
                outs.append(_bdot(wts, xp))
            pairs.append(jnp.where(lane_in_pair < hdim, outs[0], outs[1]))
        y = jnp.concatenate(pairs, axis=1)
        ecum = jnp.exp(cum)
        y = y + _bdot(cc, state_ref[...]) * _dot_sel(ecum, eh)
        last = cum[CHUNK - 1:CHUNK, :]
        to_end = jnp.exp(last - cum) * dt
        xw = xc * _dot_sel(to_end, eh)
        sdec = _dot_sel(jnp.exp(last), eh)
        state_ref[...] = state_ref[...] * sdec + _bdot_tn(bc, xw)
        ys.append(y)
    y = jnp.concatenate(ys, axis=0)
    y = y + dskip_ref[...] * xs
    z = z_ref[...].astype(F32)
    y = y * (z * jax.nn.sigmoid(z))
    y = y * lax.rsqrt(jnp.mean(y * y, axis=-1, keepdims=True) + NORM_EPS) * gn_ref[...]
    o_ref[...] = y.astype(o_ref.dtype)


def ssd_mixer_pallas(z, xbc, dt_raw, conv_w, conv_b, dt_bias, a_log, d_skip, gate_norm, bsz, seq, tb=512):
    m, c_inner = z.shape
    nh = dt_raw.shape[1]
    hpg = nh // C_GROUPS
    width = c_inner // C_GROUPS
    assert width == hpg * C_HEAD_DIM and width % LANES == 0 and C_STATE % LANES == 0 and hpg % 2 == 0
    steps = seq // tb
    dt = jax.nn.softplus(dt_raw.astype(F32) + dt_bias.astype(F32)).reshape(bsz, seq, C_GROUPS, hpg)
    da = dt * -jnp.exp(a_log.astype(F32)).reshape(C_GROUPS, hpg)
    dt_g = jnp.transpose(dt, (0, 2, 1, 3))
    da_g = jnp.transpose(da, (0, 2, 1, 3))
    dt_t = jnp.transpose(dt, (0, 2, 3, 1))
    da_t = jnp.transpose(da, (0, 2, 3, 1))
    dskip = jnp.repeat(d_skip.astype(F32), C_HEAD_DIM).reshape(C_GROUPS, 1, width)
    gn = gate_norm.astype(F32).reshape(C_GROUPS, 1, width)
    cw = conv_w.astype(F32)
    cbias = conv_b.astype(F32).reshape(1, -1)
    boff = c_inner // C_STATE
    coff = boff + C_GROUPS

    row_ix = lambda b, s: b * steps + s
    small = pl.BlockSpec((1, 1, tb, hpg), lambda b, g, s: (b, g, s, 0))
    small_t = pl.BlockSpec((1, 1, hpg, tb), lambda b, g, s: (b, g, 0, s))
    per_group = pl.BlockSpec((None, 1, width), lambda b, g, s: (g, 0, 0))
    in_specs = [
        pl.BlockSpec((tb, width), lambda b, g, s: (row_ix(b, s), g)),
        pl.BlockSpec((tb, C_STATE), lambda b, g, s: (row_ix(b, s), boff + g)),
        pl.BlockSpec((tb, C_STATE), lambda b, g, s: (row_ix(b, s), coff + g)),
        pl.BlockSpec((tb, width), lambda b, g, s: (row_ix(b, s), g)),
        pl.BlockSpec((C_CONV, width), lambda b, g, s: (0, g)),
        pl.BlockSpec((C_CONV, C_STATE), lambda b, g, s: (0, boff + g)),
        pl.BlockSpec((C_CONV, C_STATE), lambda b, g, s: (0, coff + g)),
        pl.BlockSpec((1, width), lambda b, g, s: (0, g)),
        pl.BlockSpec((1, C_STATE), lambda b, g, s: (0, boff + g)),
        pl.BlockSpec((1, C_STATE), lambda b, g, s: (0, coff + g)),
        small, small_t, small, small_t, per_group, per_group,
    ]
    return pl.pallas_call(
        functools.partial(_ssd_kernel, tb=tb, hpg=hpg, hdim=C_HEAD_DIM),
        grid=(bsz, C_GROUPS, steps),
        in_specs=in_specs,
        out_specs=pl.BlockSpec((tb, width), lambda b, g, s: (row_ix(b, s), g)),
        out_shape=jax.ShapeDtypeStruct((m, c_inner), BF16),
        scratch_shapes=[pltpu.VMEM((C_STATE, width), F32), pltpu.VMEM((CONV_TAIL, width), F32),
                        pltpu.VMEM((CONV_TAIL, C_STATE), F32), pltpu.VMEM((CONV_TAIL, C_STATE), F32)],
        compiler_params=pltpu.CompilerParams(
            dimension_semantics=("parallel", "parallel", "arbitrary"), vmem_limit_bytes=VMEM_LIMIT_BYTES),
        name="ssd",
    )(xbc, xbc, xbc, z, cw, cw, cw, cbias, cbias, cbias, dt_g, dt_t, da_g, da_t, dskip, gn)


def _dsa_rwkv_layer(x, positions, norm_mix, w_in, q_norm, w_uq, w_qi, kv_norm, w_uk, w_uv,
                    shift_mu, w_up, w0, a_up, a0, g_up, k_k, k_a, r_k, gn_w, gn_b, w_out,
                    norm_ffn, ffn_gate, ffn_up, ffn_down):
    bsz, seq, d = x.shape
    m = bsz * seq
    x2 = x.reshape(m, d)
    cos_a, sin_a = _rope_tables(positions, A_HEAD_DIM)
    cos_i, sin_i = _rope_tables(positions, IDX_DIM)

    b_in = w_in.shape[1] - A_IN
    b_col = _round_up(A_IN, 8 * B_HEAD_DIM)
    width = _round_up(b_col + b_in, 256)
    w_in_b = w_in.astype(BF16)
    w_in_p = jnp.concatenate([_pad_axis(w_in_b[:, :A_IN], 1, b_col), _pad_axis(w_in_b[:, A_IN:], 1, width - b_col)], axis=1)
    mu_p = jnp.concatenate([jnp.zeros((b_col,), F32), _pad_axis(shift_mu.astype(F32), 0, width - b_col)])
    proj = matmul_shift_pallas(rmsnorm_pallas(x2, norm_mix), w_in_p, mu_p, seq, name="l0_in").reshape(bsz, seq, width)
    c_q = proj[:, :, :A_Q_RANK]
    c_kv = proj[:, :, A_Q_RANK:A_Q_RANK + A_KV_RANK]
    k_idx = proj[:, :, A_Q_RANK + A_KV_RANK:A_Q_RANK + A_KV_RANK + IDX_DIM]
    w_idx = proj[:, :, A_Q_RANK + A_KV_RANK + IDX_DIM:A_IN]
    c_q = _rmsnorm_jax(c_q, q_norm)
    c_kv = _rmsnorm_jax(c_kv, kv_norm)
    c_q2 = c_q.reshape(m, A_Q_RANK).astype(BF16)
    rope_cs = lambda cos, sin: (jnp.concatenate([cos, cos], -1).reshape(m, -1), jnp.concatenate([-sin, sin], -1).reshape(m, -1))
    q = matmul_rope_pallas(c_q2, w_uq.astype(BF16), *rope_cs(cos_a, sin_a), scale=A_HEAD_DIM ** -0.5, name="l0_q")
    q_idx = matmul_rope_pallas(c_q2, w_qi.astype(BF16), *rope_cs(cos_i, sin_i), name="l0_qidx")
    kv = _mm3(c_kv, jnp.concatenate([w_uk, w_uv], axis=1))
    k = _apply_rope(kv[:, :, None, :A_HEAD_DIM], cos_a, sin_a)[:, :, 0]
    v = kv[:, :, A_HEAD_DIM:]
    k_idx = _apply_rope(k_idx[:, :, None, :], cos_i, sin_i)[:, :, 0]
    w_idx = w_idx * (IDX_HEADS * IDX_DIM) ** -0.5
    y_a = dsa_pallas(q.reshape(bsz, seq, -1), q_idx.reshape(bsz, seq, -1), w_idx, k, v, k_idx,
                     topk=min(IDX_TOPK_MAX, seq // 4), ha=A_HEADS, hi=IDX_HEADS)
    y_b = _rwkv7_time_mix(proj, b_col, w_up, w0, a_up, a0, g_up, k_k, k_a, r_k, gn_w, gn_b)
    y = jnp.concatenate([y_a.astype(BF16), y_b], axis=-1).reshape(m, -1)
    x2 = matmul_pallas(y, w_out.astype(BF16), res=x2, name="l0_out")

    ffn = ffn_gate.shape[1]
    ffn_p = _round_up(ffn, 512)
    h = rmsnorm_pallas(x2, norm_ffn)
    wg = _pad_axis(ffn_gate.astype(BF16), 1, ffn_p)[None]
    wu = _pad_axis(ffn_up.astype(BF16), 1, ffn_p)[None]
    wd = _pad_axis(ffn_down.astype(BF16), 0, ffn_p)
    tm = min(1024, m)
    hh = grouped_glu_pallas(h, wg, wu, jnp.zeros((m // tm,), I32), jnp.full((1,), m // tm, I32), tm=tm, name="l0_glu")
    x2 = matmul_pallas(hh, wd, res=x2, tm=1024, tn=1024, tk=ffn_p // 4, name="l0_down")
    return x2.reshape(bsz, seq, d)


def _ssd_moe_layer(x, norm_mix, w_in, conv_w, conv_b, dt_bias, a_log, d_skip, gate_norm, w_out,
                   norm_ffn, router, exp_gate, exp_up, exp_down):
    bsz, seq, d = x.shape
    m = bsz * seq
    x2 = x.reshape(m, d)
    c_inner = w_out.shape[0]
    c_conv_dim = c_inner + 2 * C_GROUPS * C_STATE
    h = rmsnorm_pallas(x2, norm_mix)
    w_in_b = w_in.astype(BF16)
    z = matmul_pallas(h, w_in_b[:, :c_inner], name="l1_in_z")
    xbc = matmul_pallas(h, w_in_b[:, c_inner:c_inner + c_conv_dim], name="l1_in_xbc")
    dt_raw = matmul_pallas(h, w_in_b[:, c_inner + c_conv_dim:], name="l1_in_dt")
    y = ssd_mixer_pallas(z, xbc, dt_raw, conv_w, conv_b, dt_bias, a_log, d_skip, gate_norm, bsz, seq)
    x2 = matmul_pallas(y, w_out.astype(BF16), res=x2, tm=512, tn=512, name="l1_out")
    return (x2,) + moe_sparse(x2, norm_ffn, router, exp_gate, exp_up, exp_down)


def kernel(x, positions, l0_norm_mix, l0_w_in, l0_q_norm, l0_w_uq, l0_w_qi, l0_kv_norm, l0_w_uk, l0_w_uv,
           l0_shift_mu, l0_w_up, l0_w0, l0_a_up, l0_a0, l0_g_up, l0_k_k, l0_k_a, l0_r_k, l0_gn_w, l0_gn_b,
           l0_w_out, l0_norm_ffn, l0_ffn_gate, l0_ffn_up, l0_ffn_down,
           l1_norm_mix, l1_w_in, l1_conv_w, l1_conv_b, l1_dt_bias, l1_a_log, l1_d_skip, l1_gate_norm, l1_w_out,
           l1_norm_ffn, l1_router, l1_exp_gate, l1_exp_up, l1_exp_down, final_norm):
    x = _dsa_rwkv_layer(x, positions, l0_norm_mix, l0_w_in, l0_q_norm, l0_w_uq, l0_w_qi, l0_kv_norm, l0_w_uk,
                        l0_w_uv, l0_shift_mu, l0_w_up, l0_w0, l0_a_up, l0_a0, l0_g_up, l0_k_k, l0_k_a, l0_r_k,
                        l0_gn_w, l0_gn_b, l0_w_out, l0_norm_ffn, l0_ffn_gate, l0_ffn_up, l0_ffn_down)
    bsz, seq, d = x.shape
    x2, y1, y2, route = _ssd_moe_layer(x, l1_norm_mix, l1_w_in, l1_conv_w, l1_conv_b, l1_dt_bias, l1_a_log, l1_d_skip,
                                       l1_gate_norm, l1_w_out, l1_norm_ffn, l1_router, l1_exp_gate, l1_exp_up,
                                       l1_exp_down)
    return combine_norm_pallas(x2, y1, y2, route, final_norm).reshape(bsz, seq, d)
```
